```python
import math
import jax, jax.numpy as jnp
from jax import lax
import numpy as np

D_MODEL = 4096
BATCH = 2
SEQ = 4096
DEPTH = 1
DEC_BATCH = 1
DEC_SEQ = 16384
PAST_LEN = 128

HEAD_DIM = 128
ROT_DIM = HEAD_DIM // 4
ROPE_THETA = 500000.0
DIL_GROUPS = ((128, 1), (512, 4), (2048, 16))
N_DIL = len(DIL_GROUPS)
A_HEADS = 8
A_BLOCK = 64
A_WIDTH = A_HEADS * HEAD_DIM
B_Q_HEADS = 16
B_KV_HEADS = 4
B_WINDOW = 128
B_Q_WIDTH = B_Q_HEADS * HEAD_DIM
B_KV_WIDTH = B_KV_HEADS * HEAD_DIM
D_FF = 11008
EPS = 1e-6
NEG_INF = -1e30
IN_SPLITS = (N_DIL * A_WIDTH, N_DIL * A_WIDTH, N_DIL * A_WIDTH,
             B_Q_WIDTH, B_KV_WIDTH, B_KV_WIDTH, D_MODEL, D_MODEL)
IN_WIDTH = sum(IN_SPLITS)

kernel_name = "hybrid_dilated_swa_gqa_sink_encoder"


def rmsnorm(x, g):
    xf = x.astype(jnp.float32)
    var = jnp.mean(xf * xf, axis=-1, keepdims=True)
    return (xf * lax.rsqrt(var + EPS)).astype(x.dtype) * g


def swiglu(x, w_gate, w_up, w_down):
    return (jax.nn.silu(x @ w_gate) * (x @ w_up)) @ w_down


def partial_rope(x, pos):
    half = ROT_DIM // 2
    inv = ROPE_THETA ** (-jnp.arange(half, dtype=jnp.float32) / half)
    ang = pos.astype(jnp.float32)[:, None] * inv[None, :]
    cos = jnp.cos(ang)[None, :, None, :]
    sin = jnp.sin(ang)[None, :, None, :]
    xr = x[..., :ROT_DIM].astype(jnp.float32)
    x1, x2 = xr[..., :half], xr[..., half:]
    rot = jnp.concatenate([x1 * cos - x2 * sin, x2 * cos + x1 * sin], axis=-1)
    return jnp.concatenate([rot.astype(x.dtype), x[..., ROT_DIM:]], axis=-1)


def banded_attention(q, k, v, half_window, blk, sink_logit=None):
    b, L, hq, dh = q.shape
    hk = k.shape[2]
    grp = hq // hk
    nb = -(-L // blk)
    pad = nb * blk - L
    qp = jnp.pad(q, ((0, 0), (0, pad), (0, 0), (0, 0)))
    kp = jnp.pad(k, ((0, 0), (blk, pad + blk), (0, 0), (0, 0)))
    vp = jnp.pad(v, ((0, 0), (blk, pad + blk), (0, 0), (0, 0)))
    qb = qp.reshape(b, nb, blk, hk, grp, dh)
    kb = kp.reshape(b, nb + 2, blk, hk, dh)
    vb = vp.reshape(b, nb + 2, blk, hk, dh)
    kn = jnp.concatenate([kb[:, :-2], kb[:, 1:-1], kb[:, 2:]], axis=2)
    vn = jnp.concatenate([vb[:, :-2], vb[:, 1:-1], vb[:, 2:]], axis=2)
    scale = 1.0 / math.sqrt(dh)
    s = jnp.einsum('bnqhgd,bnkhd->bnhgqk', qb, kn,
                   preferred_element_type=jnp.float32) * scale
    qpos = jnp.arange(nb)[:, None] * blk + jnp.arange(blk)[None, :]
    kpos = (jnp.arange(nb) * blk - blk)[:, None] + jnp.arange(3 * blk)[None, :]
    rel = kpos[:, None, :] - qpos[:, :, None]
    valid = (jnp.abs(rel) <= half_window) & (kpos[:, None, :] >= 0) & (kpos[:, None, :] < L)
    s = jnp.where(valid[None, :, None, None], s, NEG_INF)
    m = jnp.max(s, axis=-1, keepdims=True)
    if sink_logit is not None:
        sk = sink_logit.astype(jnp.float32).reshape(hk, grp)[None, None, :, :, None, None]
        m = jnp.maximum(m, sk)
    p = jnp.exp(s - m)
    den = jnp.sum(p, axis=-1, keepdims=True)
    if sink_logit is not None:
        den = den + jnp.exp(sk - m)
    o = jnp.einsum('bnhgqk,bnkhd->bnqhgd', p.astype(v.dtype), vn,
                   preferred_element_type=jnp.float32)
    den_t = jnp.transpose(den[..., 0], (0, 1, 4, 2, 3))[..., None]
    o = (o / den_t).astype(q.dtype).reshape(b, nb * blk, hq, dh)[:, :L]
    lse = jnp.transpose((m + jnp.log(den))[..., 0], (0, 1, 4, 2, 3)).reshape(b, nb * blk, hq)[:, :L]
    return o, lse


def dilated_attention(q, k, v, window, dilation):
    b, S, h, dh = q.shape
    L = S // dilation

    def to_sub(t):
        return t.reshape(b, L, dilation, h, dh).transpose(0, 2, 1, 3, 4).reshape(b * dilation, L, h, dh)

    o, lse = banded_attention(to_sub(q), to_sub(k), to_sub(v), (window // 2) // dilation, A_BLOCK)
    o = o.reshape(b, dilation, L, h, dh).transpose(0, 2, 1, 3, 4).reshape(b, S, h, dh)
    lse = lse.reshape(b, dilation, L, h).transpose(0, 2, 1, 3).reshape(b, S, h)
    return o, lse


def encoder_layer(x, g_ffn1, w1_gate, w1_up, w1_down, g_mix, w_in, sink_b,
                  w_branch_a, w_branch_b, w_out, g_ffn2, w2_gate, w2_up, w2_down):
    b, S, _ = x.shape
    pos = jnp.arange(S)
    h = x + 0.5 * swiglu(rmsnorm(x, g_ffn1), w1_gate, w1_up, w1_down)
    n = rmsnorm(h, g_mix)
    proj = n @ w_in
    qa, ka, va, qb, kb, vb, ga, gb = jnp.split(proj, np.cumsum(IN_SPLITS)[:-1].tolist(), axis=-1)
    qa = partial_rope(qa.reshape(b, S, N_DIL * A_HEADS, HEAD_DIM), pos).reshape(b, S, N_DIL, A_HEADS, HEAD_DIM)
    ka = partial_rope(ka.reshape(b, S, N_DIL * A_HEADS, HEAD_DIM), pos).reshape(b, S, N_DIL, A_HEADS, HEAD_DIM)
    va = va.reshape(b, S, N_DIL, A_HEADS, HEAD_DIM)
    outs, lses = [], []
    for gi, (win, dil) in enumerate(DIL_GROUPS):
        o, l = dilated_attention(qa[:, :, gi], ka[:, :, gi], va[:, :, gi], win, dil)
        outs.append(o)
        lses.append(l)
    alpha = jax.nn.softmax(jnp.stack(lses, axis=0), axis=0)
    ya = jnp.sum(alpha[..., None] * jnp.stack(outs, axis=0).astype(jnp.float32), axis=0)
    ya = ya.astype(x.dtype).reshape(b, S, A_WIDTH)
    qb = partial_rope(qb.reshape(b, S, B_Q_HEADS, HEAD_DIM), pos)
    kb = partial_rope(kb.reshape(b, S, B_KV_HEADS, HEAD_DIM), pos)
    vb = vb.reshape(b, S, B_KV_HEADS, HEAD_DIM)
    yb, _ = banded_attention(qb, kb, vb, B_WINDOW, B_WINDOW, sink_logit=sink_b)
    yb = yb.reshape(b, S, B_Q_WIDTH)
    merged = jax.nn.sigmoid(ga) * (ya @ w_branch_a) + jax.nn.sigmoid(gb) * (yb @ w_branch_b)
    h2 = h + merged @ w_out
    return h2 + 0.5 * swiglu(rmsnorm(h2, g_ffn2), w2_gate, w2_up, w2_down)


def trunk(x, g_ffn1, w1_gate, w1_up, w1_down, g_mix, w_in, sink_b, w_branch_a, w_branch_b,
          w_out, g_ffn2, w2_gate, w2_up, w2_down, g_final):
    for l in range(DEPTH):
        x = encoder_layer(x, g_ffn1[l], w1_gate[l], w1_up[l], w1_down[l], g_mix[l], w_in[l], sink_b[l],
                          w_branch_a[l], w_branch_b[l], w_out[l], g_ffn2[l], w2_gate[l], w2_up[l], w2_down[l])
    return rmsnorm(x, g_final)


def setup_inputs(seed: int = 0) -> dict:
    key = jax.random.key(seed)
    ks = jax.random.split(key, 20)
    f32 = jnp.float32

    def w(k, shape, fan_in):
        return jax.random.normal(k, shape, f32) * (fan_in ** -0.5)

    def gain(k, shape):
        return 1.0 + 0.05 * jax.random.normal(k, shape, f32)

    return {
        "x_prompt": jax.random.normal(ks[0], (BATCH, SEQ, D_MODEL), f32),
        "x_sample": jax.random.normal(ks[1], (DEC_BATCH, DEC_SEQ, D_MODEL), f32),
        "g_ffn1": gain(ks[2], (DEPTH, D_MODEL)),
        "w1_gate": w(ks[3], (DEPTH, D_MODEL, D_FF), D_MODEL),
        "w1_up": w(ks[4], (DEPTH, D_MODEL, D_FF), D_MODEL),
        "w1_down": w(ks[5], (DEPTH, D_FF, D_MODEL), D_FF),
        "g_mix": gain(ks[6], (DEPTH, D_MODEL)),
        "w_in": w(ks[7], (DEPTH, D_MODEL, IN_WIDTH), D_MODEL),
        "sink_b": 0.5 * jax.random.normal(ks[8], (DEPTH, B_Q_HEADS), f32),
        "w_branch_a": w(ks[9], (DEPTH, A_WIDTH, D_MODEL), A_WIDTH),
        "w_branch_b": w(ks[10], (DEPTH, B_Q_WIDTH, D_MODEL), B_Q_WIDTH),
        "w_out": w(ks[11], (DEPTH, D_MODEL, D_MODEL), D_MODEL),
        "g_ffn2": gain(ks[12], (DEPTH, D_MODEL)),
        "w2_gate": w(ks[13], (DEPTH, D_MODEL, D_FF), D_MODEL),
        "w2_up": w(ks[14], (DEPTH, D_MODEL, D_FF), D_MODEL),
        "w2_down": w(ks[15], (DEPTH, D_FF, D_MODEL), D_FF),
        "g_final": gain(ks[16], (D_MODEL,)),
    }


def reference(x_prompt, x_sample, g_ffn1, w1_gate, w1_up, w1_down, g_mix, w_in, sink_b,
              w_branch_a, w_branch_b, w_out, g_ffn2, w2_gate, w2_up, w2_down, g_final):
    y_prompt = trunk(x_prompt, g_ffn1, w1_gate, w1_up, w1_down, g_mix, w_in, sink_b, w_branch_a,
                     w_branch_b, w_out, g_ffn2, w2_gate, w2_up, w2_down, g_final)
    y_sample = trunk(x_sample, g_ffn1, w1_gate, w1_up, w1_down, g_mix, w_in, sink_b, w_branch_a,
                     w_branch_b, w_out, g_ffn2, w2_gate, w2_up, w2_down, g_final)
    return (y_prompt, y_sample)
```

```python
import functools
import math

import jax
import jax.numpy as jnp
from jax import lax
from jax.experimental import pallas as pl
from jax.experimental.pallas import tpu as pltpu

F32 = jnp.float32
BF16 = jnp.bfloat16

D_MODEL = 4096
D_FF = 11008
HEAD_DIM = 128
ROT_DIM = HEAD_DIM // 4
ROPE_THETA = 500000.0
DIL_GROUPS = ((128, 1), (512, 4), (2048, 16))
A_HEADS = 8
A_WIDTH = A_HEADS * HEAD_DIM
B_Q_HEADS = 16
B_KV_HEADS = 4
B_GROUP = B_Q_HEADS // B_KV_HEADS
B_WINDOW = 128
A_HALF = 64
EPS = 1e-6
NEG_INF = -1e30
SCALE = 1.0 / math.sqrt(HEAD_DIM)

SEQ_STARTS = (0, 4096, 8192)
SEQ_ENDS = (4096, 8192, 24576)
M_TOKENS = 24576

IN_WIDTH = 20480
QA_OFF, KA_OFF, QB_OFF, VA_OFF, KB_OFF, VB_OFF, GA_OFF, GB_OFF = (
    0, 3072, 6144, 8192, 11264, 11776, 12288, 16384)

D_FF_PAD = 11264

VMEM_LIMIT = 58 * 1024 * 1024


def _params(n_grid):
    return pltpu.CompilerParams(
        dimension_semantics=("arbitrary",) * n_grid,
        vmem_limit_bytes=VMEM_LIMIT)


def _rmsnorm_kernel(x_ref, g_ref, o_ref):
    x = x_ref[...]
    var = jnp.mean(x * x, axis=-1, keepdims=True)
    o_ref[...] = ((x * lax.rsqrt(var + EPS)) * g_ref[...]).astype(o_ref.dtype)


def _rmsnorm(x, g, out_dtype, tm=256):
    m, d = x.shape
    return pl.pallas_call(
        _rmsnorm_kernel,
        grid=(m // tm,),
        in_specs=[pl.BlockSpec((tm, d), lambda i: (i, 0)),
                  pl.BlockSpec((1, d), lambda i: (0, 0))],
        out_specs=pl.BlockSpec((tm, d), lambda i: (i, 0)),
        out_shape=jax.ShapeDtypeStruct((m, d), out_dtype),
        compiler_params=_params(1),
        name="rmsnorm",
    )(x, g.reshape(1, d))


def _ffn_up_kernel(x_ref, wg_ref, wu_ref, o_ref):
    x = x_ref[...]
    g = jnp.dot(x, wg_ref[...], preferred_element_type=F32)
    u = jnp.dot(x, wu_ref[...], preferred_element_type=F32)
    o_ref[...] = (g * jax.nn.sigmoid(g) * u).astype(o_ref.dtype)


def _ffn_up(xn, wg, wu, tm=1024, tn=512):
    m, k = xn.shape
    n = wg.shape[1]
    return pl.pallas_call(
        _ffn_up_kernel,
        grid=(m // tm, n // tn),
        in_specs=[pl.BlockSpec((tm, k), lambda i, j: (i, 0)),
                  pl.BlockSpec((k, tn), lambda i, j: (0, j)),
                  pl.BlockSpec((k, tn), lambda i, j: (0, j))],
        out_specs=pl.BlockSpec((tm, tn), lambda i, j: (i, j)),
        out_shape=jax.ShapeDtypeStruct((m, n), BF16),
        compiler_params=_params(2),
        name="ffn_up",
    )(xn, wg, wu)


def _mm_residual_kernel(x_ref, w_ref, r_ref, o_ref, *, scale):
    acc = jnp.dot(x_ref[...], w_ref[...], preferred_element_type=F32)
    o_ref[...] = r_ref[...] + scale * acc


def _mm_residual(x, w, res, scale, tm, tn, single_buffer_x=False):
    m, k = x.shape
    n = w.shape[1]
    if single_buffer_x:
        x_spec = pl.BlockSpec((tm, k), lambda i, j: (i, 0), pipeline_mode=pl.Buffered(1))
    else:
        x_spec = pl.BlockSpec((tm, k), lambda i, j: (i, 0))
    return pl.pallas_call(
        functools.partial(_mm_residual_kernel, scale=scale),
        grid=(m // tm, n // tn),
        in_specs=[x_spec,
                  pl.BlockSpec((k, tn), lambda i, j: (0, j)),
                  pl.BlockSpec((tm, tn), lambda i, j: (i, j))],
        out_specs=pl.BlockSpec((tm, tn), lambda i, j: (i, j)),
        out_shape=jax.ShapeDtypeStruct((m, n), F32),
        compiler_params=_params(2),
        name="mm_residual",
    )(x, w, res)


IN_TN = 1024
ROPE_BLOCKS = VA_OFF // IN_TN
PLAIN_BLOCKS_END = KB_OFF // IN_TN
MIXED_BLOCK = KB_OFF // IN_TN
GATE_BLOCKS_START = GA_OFF // IN_TN


def _rope_head(seg, cos, sin_lo, sin_hi):
    up = pltpu.roll(seg, HEAD_DIM - ROT_DIM // 2, axis=1)
    down = pltpu.roll(seg, ROT_DIM // 2, axis=1)
    return seg * cos + up * sin_lo + down * sin_hi


def _in_proj_kernel(x_ref, w_ref, cos_ref, slo_ref, shi_ref, o_ref):
    j = pl.program_id(1)
    acc = jnp.dot(x_ref[...], w_ref[...], preferred_element_type=F32)
    heads = IN_TN // HEAD_DIM

    def rope_store(h):
        sl = slice(h * HEAD_DIM, (h + 1) * HEAD_DIM)
        o_ref[:, sl] = _rope_head(acc[:, sl], cos_ref[...], slo_ref[...], shi_ref[...]).astype(o_ref.dtype)

    @pl.when(j < ROPE_BLOCKS)
    def _():
        for h in range(heads):
            rope_store(h)

    @pl.when((j >= ROPE_BLOCKS) & (j < PLAIN_BLOCKS_END))
    def _():
        o_ref[...] = acc.astype(o_ref.dtype)

    @pl.when(j == MIXED_BLOCK)
    def _():
        for h in range(heads // 2):
            rope_store(h)
        half = IN_TN // 2
        o_ref[:, half:] = acc[:, half:].astype(o_ref.dtype)

    @pl.when(j >= GATE_BLOCKS_START)
    def _():
        o_ref[...] = jax.nn.sigmoid(acc).astype(o_ref.dtype)


def _in_proj(xn, w, cos_t, slo_t, shi_t, tm=1024):
    m, k = xn.shape
    n = w.shape[1]
    tn = IN_TN
    tab = pl.BlockSpec((tm, HEAD_DIM), lambda i, j: (i, 0))
    return pl.pallas_call(
        _in_proj_kernel,
        grid=(m // tm, n // tn),
        in_specs=[pl.BlockSpec((tm, k), lambda i, j: (i, 0)),
                  pl.BlockSpec((k, tn), lambda i, j: (0, j)),
                  tab, tab, tab],
        out_specs=pl.BlockSpec((tm, tn), lambda i, j: (i, j)),
        out_shape=jax.ShapeDtypeStruct((m, n), BF16),
        compiler_params=_params(2),
        name="in_proj",
    )(xn, w, cos_t, slo_t, shi_t)


def _seq_bounds(g0):
    lo = jnp.where(g0 < SEQ_ENDS[0], SEQ_STARTS[0], jnp.where(g0 < SEQ_ENDS[1], SEQ_STARTS[1], SEQ_STARTS[2]))
    hi = jnp.where(g0 < SEQ_ENDS[0], SEQ_ENDS[0], jnp.where(g0 < SEQ_ENDS[1], SEQ_ENDS[1], SEQ_ENDS[2]))
    return lo, hi


def _band_mask(tq, nk, halo, half_window, lo, hi, u0):
    ii = lax.broadcasted_iota(jnp.int32, (tq, nk), 0)
    jj = lax.broadcasted_iota(jnp.int32, (tq, nk), 1)
    rel = jj - ii - halo
    return ((rel >= -half_window) & (rel <= half_window)
            & (jj >= lo - u0 + halo) & (jj < hi - u0 + halo))


A_TQ = 128


def _attn_a_kernel(q_ref, kp_ref, kc_ref, kn_ref, vp_ref, vc_ref, vn_ref, o_ref, l_ref, *, dil):
    t = pl.program_id(1)
    u0 = t * A_TQ
    lo, hi = _seq_bounds(u0 * dil)
    nk = A_TQ + 2 * A_HALF
    valid = _band_mask(A_TQ, nk, A_HALF, A_HALF, lo // dil, hi // dil, u0)
    k = jnp.concatenate([kp_ref[...], kc_ref[...], kn_ref[...]], axis=0)
    v = jnp.concatenate([vp_ref[...], vc_ref[...], vn_ref[...]], axis=0)
    for h in range(A_HEADS):
        sl = slice(h * HEAD_DIM, (h + 1) * HEAD_DIM)
        s = lax.dot_general(q_ref[:, sl], k[:, sl], (((1,), (1,)), ((), ())),
                            preferred_element_type=F32) * SCALE
        s = jnp.where(valid, s, NEG_INF)
        m = jnp.max(s, axis=-1, keepdims=True)
        p = jnp.exp(s - m)
        den = jnp.sum(p, axis=-1, keepdims=True)
        o = jnp.dot(p.astype(BF16), v[:, sl], preferred_element_type=F32)
        o_ref[:, sl] = o * (1.0 / den)
        l_ref[:, sl] = jnp.broadcast_to(m + jnp.log(den), (A_TQ, HEAD_DIM))


def _attn_a(proj, group, dil):
    m = proj.shape[0]
    rows = m // dil
    pv = proj.reshape(rows, dil * IN_WIDTH)
    cols = IN_WIDTH // A_WIDTH
    hb = A_TQ // A_HALF
    last_halo = rows // A_HALF - 1

    def main(off):
        return pl.BlockSpec((A_TQ, A_WIDTH), lambda r, t: (t, r * cols + off // A_WIDTH + group))

    def prev(off):
        return pl.BlockSpec((A_HALF, A_WIDTH),
                            lambda r, t: (jnp.maximum(t * hb - 1, 0), r * cols + off // A_WIDTH + group))

    def nxt(off):
        return pl.BlockSpec((A_HALF, A_WIDTH),
                            lambda r, t: (jnp.minimum(t * hb + hb, last_halo), r * cols + off // A_WIDTH + group))

    out_spec = pl.BlockSpec((A_TQ, A_WIDTH), lambda r, t: (t, r))
    out_shape = jax.ShapeDtypeStruct((rows, dil * A_WIDTH), F32)
    o, l = pl.pallas_call(
        functools.partial(_attn_a_kernel, dil=dil),
        grid=(dil, rows // A_TQ),
        in_specs=[main(QA_OFF), prev(KA_OFF), main(KA_OFF), nxt(KA_OFF),
                  prev(VA_OFF), main(VA_OFF), nxt(VA_OFF)],
        out_specs=[out_spec, out_spec],
        out_shape=[out_shape, out_shape],
        compiler_params=_params(2),
        name=f"attn_a_d{dil}",
    )(pv, pv, pv, pv, pv, pv, pv)
    return o.reshape(m, A_WIDTH), l.reshape(m, A_WIDTH)


def _combine_a_kernel(o0, o1, o2, l0, l1, l2, y_ref):
    a0, a1, a2 = l0[...], l1[...], l2[...]
    mx = jnp.maximum(jnp.maximum(a0, a1), a2)
    e0, e1, e2 = jnp.exp(a0 - mx), jnp.exp(a1 - mx), jnp.exp(a2 - mx)
    inv = 1.0 / (e0 + e1 + e2)
    y_ref[...] = ((e0 * inv) * o0[...] + (e1 * inv) * o1[...] + (e2 * inv) * o2[...]).astype(y_ref.dtype)


def _combine_a(outs, lses, tm=512):
    m = outs[0].shape[0]
    spec = pl.BlockSpec((tm, A_WIDTH), lambda i: (i, 0))
    return pl.pallas_call(
        _combine_a_kernel,
        grid=(m // tm,),
        in_specs=[spec] * 6,
        out_specs=spec,
        out_shape=jax.ShapeDtypeStruct((m, A_WIDTH), BF16),
        compiler_params=_params(1),
        name="combine_a",
    )(*outs, *lses)


B_TQ = 128


def _attn_b_kernel(sink_ref, q_ref, kp_ref, kc_ref, kn_ref, vp_ref, vc_ref, vn_ref, o_ref):
    t = pl.program_id(0)
    u0 = t * B_TQ
    lo, hi = _seq_bounds(u0)
    nk = 3 * B_TQ
    valid = _band_mask(B_TQ, nk, B_TQ, B_WINDOW, lo, hi, u0)
    k = jnp.concatenate([kp_ref[...], kc_ref[...], kn_ref[...]], axis=0)
    v = jnp.concatenate([vp_ref[...], vc_ref[...], vn_ref[...]], axis=0)
    for h in range(B_Q_HEADS):
        sl = slice(h * HEAD_DIM, (h + 1) * HEAD_DIM)
        g = h // B_GROUP
        gl = slice(g * HEAD_DIM, (g + 1) * HEAD_DIM)
        sk = sink_ref[h]
        s = lax.dot_general(q_ref[:, sl], k[:, gl], (((1,), (1,)), ((), ())),
                            preferred_element_type=F32) * SCALE
        s = jnp.where(valid, s, NEG_INF)
        m = jnp.maximum(jnp.max(s, axis=-1, keepdims=True), sk)
        p = jnp.exp(s - m)
        den = jnp.sum(p, axis=-1, keepdims=True) + jnp.exp(sk - m)
        o = jnp.dot(p.astype(BF16), v[:, gl], preferred_element_type=F32)
        o_ref[:, sl] = (o * (1.0 / den)).astype(o_ref.dtype)


def _attn_b(proj, sink):
    m = proj.shape[0]
    n_tiles = m // B_TQ
    qw = B_Q_HEADS * HEAD_DIM
    kw = B_KV_HEADS * HEAD_DIM

    def kv(off, shift):
        return pl.BlockSpec((B_TQ, kw), lambda t: (jnp.clip(t + shift, 0, n_tiles - 1), off // kw))

    return pl.pallas_call(
        _attn_b_kernel,
        grid=(n_tiles,),
        in_specs=[pl.BlockSpec(memory_space=pltpu.SMEM),
                  pl.BlockSpec((B_TQ, qw), lambda t: (t, QB_OFF // qw)),
                  kv(KB_OFF, -1), kv(KB_OFF, 0), kv(KB_OFF, 1),
                  kv(VB_OFF, -1), kv(VB_OFF, 0), kv(VB_OFF, 1)],
        out_specs=pl.BlockSpec((B_TQ, qw), lambda t: (t, 0)),
        out_shape=jax.ShapeDtypeStruct((m, qw), BF16),
        compiler_params=_params(1),
        name="attn_b",
    )(sink, proj, proj, proj, proj, proj, proj, proj)


def _merge_kernel(ya_ref, yb_ref, wa_ref, wb_ref, ga_ref, gb_ref, o_ref):
    a = jnp.dot(ya_ref[...], wa_ref[...], preferred_element_type=F32)
    b = jnp.dot(yb_ref[...], wb_ref[...], preferred_element_type=F32)
    o_ref[...] = (ga_ref[...].astype(F32) * a + gb_ref[...].astype(F32) * b).astype(o_ref.dtype)


def _merge(ya, yb, wa, wb, proj, tm=1024, tn=1024):
    m = ya.shape[0]
    n = wa.shape[1]
    return pl.pallas_call(
        _merge_kernel,
        grid=(m // tm, n // tn),
        in_specs=[pl.BlockSpec((tm, ya.shape[1]), lambda i, j: (i, 0)),
                  pl.BlockSpec((tm, yb.shape[1]), lambda i, j: (i, 0)),
                  pl.BlockSpec((wa.shape[0], tn), lambda i, j: (0, j)),
                  pl.BlockSpec((wb.shape[0], tn), lambda i, j: (0, j)),
                  pl.BlockSpec((tm, tn), lambda i, j: (i, GA_OFF // tn + j)),
                  pl.BlockSpec((tm, tn), lambda i, j: (i, GB_OFF // tn + j))],
        out_specs=pl.BlockSpec((tm, tn), lambda i, j: (i, j)),
        out_shape=jax.ShapeDtypeStruct((m, n), BF16),
        compiler_params=_params(2),
        name="merge",
    )(ya, yb, wa, wb, proj, proj)


def _rope_tables():
    half = ROT_DIM // 2
    pos = jnp.concatenate([jnp.arange(e - s) for s, e in zip(SEQ_STARTS, SEQ_ENDS)])
    inv = ROPE_THETA ** (-jnp.arange(half, dtype=F32) / half)
    ang = pos.astype(F32)[:, None] * inv[None, :]
    cos, sin = jnp.cos(ang), jnp.sin(ang)
    m = pos.shape[0]
    zeros = jnp.zeros((m, HEAD_DIM - ROT_DIM), F32)
    zh = jnp.zeros((m, half), F32)
    cos_t = jnp.concatenate([cos, cos, jnp.ones((m, HEAD_DIM - ROT_DIM), F32)], axis=1)
    sin_lo = jnp.concatenate([-sin, zh, zeros], axis=1)
    sin_hi = jnp.concatenate([zh, sin, zeros], axis=1)
    return cos_t, sin_lo, sin_hi


def _pad_ff_cols(w):
    return jnp.pad(w.astype(BF16), ((0, 0), (0, D_FF_PAD - D_FF)))


def _pad_ff_rows(w):
    return jnp.pad(w.astype(BF16), ((0, D_FF_PAD - D_FF), (0, 0)))


def _ffn_half_step(x, g, w_gate, w_up, w_down):
    xn = _rmsnorm(x, g, BF16)
    act = _ffn_up(xn, _pad_ff_cols(w_gate), _pad_ff_cols(w_up))
    return _mm_residual(act, _pad_ff_rows(w_down), x, 0.5, tm=1024, tn=256, single_buffer_x=True)


def kernel(x_prompt, x_sample, g_ffn1, w1_gate, w1_up, w1_down, g_mix, w_in, sink_b, w_branch_a, w_branch_b, w_out, g_ffn2, w2_gate, w2_up, w2_down, g_final):
    assert w1_gate.shape[0] == 1, "single layer"
    x = jnp.concatenate([x_prompt.reshape(-1, D_MODEL), x_sample.reshape(-1, D_MODEL)], axis=0)
    assert x.shape[0] == M_TOKENS

    h = _ffn_half_step(x, g_ffn1[0], w1_gate[0], w1_up[0], w1_down[0])

    n = _rmsnorm(h, g_mix[0], BF16)
    w = w_in[0]
    w_perm = jnp.concatenate([w[:, :6144], w[:, 9216:11264], w[:, 6144:9216], w[:, 11264:]], axis=1).astype(BF16)
    proj = _in_proj(n, w_perm, *_rope_tables())

    outs, lses = [], []
    for gi, (_, dil) in enumerate(DIL_GROUPS):
        o, l = _attn_a(proj, gi, dil)
        outs.append(o)
        lses.append(l)
    ya = _combine_a(outs, lses)
    yb = _attn_b(proj, sink_b[0])

    merged = _merge(ya, yb, w_branch_a[0].astype(BF16), w_branch_b[0].astype(BF16), proj)
    h2 = _mm_residual(merged, w_out[0].astype(BF16), h, 1.0, tm=1024, tn=1024)

    x2 = _ffn_half_step(h2, g_ffn2[0], w2_gate[0], w2_up[0], w2_down[0])
    y = _rmsnorm(x2, g_final, F32)

    n_prompt = x_prompt.shape[0] * x_prompt.shape[1]
    return (y[:n_prompt].reshape(x_prompt.shape), y[n_prompt:].reshape(x_sample.shape))
```

```python
import functools
import math
from typing import NamedTuple

import jax
import jax.numpy as jnp
from jax import lax
from jax.experimental import pallas as pl
from jax.experimental.pallas import tpu as pltpu

F32 = jnp.float32
BF16 = jnp.bfloat16

D_MODEL = 4096
D_FF = 11008
HEAD_DIM = 128
ROT_DIM = HEAD_DIM // 4
ROPE_THETA = 500000.0
DIL_GROUPS = ((128, 1), (512, 4), (2048, 16))
A_HEADS = 8
A_WIDTH = A_HEADS * HEAD_DIM
A_HALF = 64
B_Q_HEADS = 16
B_KV_HEADS = 4
B_GROUP = B_Q_HEADS // B_KV_HEADS
B_Q_WIDTH = B_Q_HEADS * HEAD_DIM
B_KV_WIDTH = B_KV_HEADS * HEAD_DIM
B_WINDOW = 128
EPS = 1e-6
NEG_INF = -1e30
SCALE = 1.0 / math.sqrt(HEAD_DIM)
assert all((win // 2) // dil == A_HALF for win, dil in DIL_GROUPS)

VMEM_LIMIT = 58 * 1024 * 1024


class Geometry(NamedTuple):
    starts: tuple
    ends: tuple

    @property
    def m(self):
        return self.ends[-1]


def _params(n_grid):
    return pltpu.CompilerParams(
        dimension_semantics=("arbitrary",) * n_grid,
        vmem_limit_bytes=VMEM_LIMIT)


def _normalise(x, g):
    var = jnp.mean(x * x, axis=-1, keepdims=True)
    return (x * lax.rsqrt(var + EPS)) * g


def _rmsnorm_kernel(x_ref, g_ref, o_ref):
    o_ref[...] = _normalise(x_ref[...], g_ref[...]).astype(o_ref.dtype)


def _rmsnorm(x, g, out_dtype, tm=256):
    m, d = x.shape
    return pl.pallas_call(
        _rmsnorm_kernel,
        grid=(m // tm,),
        in_specs=[pl.BlockSpec((tm, d), lambda i: (i, 0)),
                  pl.BlockSpec((1, d), lambda i: (0, 0))],
        out_specs=pl.BlockSpec((tm, d), lambda i: (i, 0)),
        out_shape=jax.ShapeDtypeStruct((m, d), out_dtype),
        compiler_params=_params(1),
        name="rmsnorm",
    )(x, g.reshape(1, d))


NORM_TM = 256


def _plane_permutation(tm, dil):
    per = tm // dil
    assert per & (per - 1) == 0, "rows per plane must be a power of two"
    i = lax.broadcasted_iota(jnp.int32, (tm, tm), 0)
    j = lax.broadcasted_iota(jnp.int32, (tm, tm), 1)
    src = (i & (per - 1)) * dil + (i >> (per.bit_length() - 1))
    return (j == src).astype(BF16)


def _rmsnorm_planes_kernel(x_ref, g_ref, n_ref, n4_ref, n16_ref):
    y = _normalise(x_ref[...], g_ref[...]).astype(BF16)
    n_ref[...] = y
    for ref, dil in ((n4_ref, DIL_GROUPS[1][1]), (n16_ref, DIL_GROUPS[2][1])):
        perm = jnp.dot(_plane_permutation(NORM_TM, dil), y, preferred_element_type=F32).astype(BF16)
        ref[...] = perm.reshape(dil, NORM_TM // dil, y.shape[1])


def _rmsnorm_planes(x, g):
    m, d = x.shape
    tm = NORM_TM
    d4, d16 = DIL_GROUPS[1][1], DIL_GROUPS[2][1]
    n, n4, n16 = pl.pallas_call(
        _rmsnorm_planes_kernel,
        grid=(m // tm,),
        in_specs=[pl.BlockSpec((tm, d), lambda i: (i, 0)),
                  pl.BlockSpec((1, d), lambda i: (0, 0))],
        out_specs=[pl.BlockSpec((tm, d), lambda i: (i, 0)),
                   pl.BlockSpec((d4, tm // d4, d), lambda i: (0, i, 0)),
                   pl.BlockSpec((d16, tm // d16, d), lambda i: (0, i, 0))],
        out_shape=[jax.ShapeDtypeStruct((m, d), BF16),
                   jax.ShapeDtypeStruct((d4, m // d4, d), BF16),
                   jax.ShapeDtypeStruct((d16, m // d16, d), BF16)],
        compiler_params=_params(1),
        name="rmsnorm_planes",
    )(x, g.reshape(1, d))
    return n, n4.reshape(m, d), n16.reshape(m, d)


def _ffn_up_kernel(x_ref, wg_ref, wu_ref, *rest):
    o_ref = rest[-1]
    x = x_ref[...]
    g = jnp.dot(x, wg_ref[...], preferred_element_type=F32)
    u = jnp.dot(x, wu_ref[...], preferred_element_type=F32)
    o_ref[...] = (g * jax.nn.sigmoid(g) * u).astype(o_ref.dtype)


FFN_TM = 1024
FFN_TN = 512
FFN_TAIL = D_FF % FFN_TN
assert FFN_TAIL % HEAD_DIM == 0 and (D_FF - FFN_TAIL) % FFN_TAIL == 0


def _ffn_up(xn, wg, wu):
    m, k = xn.shape
    n = wg.shape[1]
    tm = FFN_TM
    x_spec = pl.BlockSpec((tm, k), lambda i, j: (i, 0))
    act = pl.pallas_call(
        _ffn_up_kernel,
        grid=(m // tm, n // FFN_TN),
        in_specs=[x_spec,
                  pl.BlockSpec((k, FFN_TN), lambda i, j: (0, j)),
                  pl.BlockSpec((k, FFN_TN), lambda i, j: (0, j))],
        out_specs=pl.BlockSpec((tm, FFN_TN), lambda i, j: (i, j)),
        out_shape=jax.ShapeDtypeStruct((m, n), BF16),
        compiler_params=_params(2),
        name="ffn_up",
    )(xn, wg, wu)
    tail_block = (n - FFN_TAIL) // FFN_TAIL
    w_tail = pl.BlockSpec((k, FFN_TAIL), lambda i, j: (0, tail_block))
    return pl.pallas_call(
        _ffn_up_kernel,
        grid=(m // tm, 1),
        in_specs=[x_spec, w_tail, w_tail, pl.BlockSpec(memory_space=pl.ANY)],
        out_specs=pl.BlockSpec((tm, FFN_TAIL), lambda i, j: (i, tail_block)),
        out_shape=jax.ShapeDtypeStruct((m, n), BF16),
        input_output_aliases={3: 0},
        compiler_params=_params(2),
        name="ffn_up_tail",
    )(xn, wg, wu, act)


def _mm_residual_kernel(x_ref, w_ref, r_ref, o_ref, *, scale):
    acc = jnp.dot(x_ref[...], w_ref[...], preferred_element_type=F32)
    o_ref[...] = r_ref[...] + scale * acc


def _mm_residual(x, w, res, scale, tm, tn, single_buffer_x=False):
    m, k = x.shape
    n = w.shape[1]
    if single_buffer_x:
        x_spec = pl.BlockSpec((tm, k), lambda i, j: (i, 0), pipeline_mode=pl.Buffered(1))
    else:
        x_spec = pl.BlockSpec((tm, k), lambda i, j: (i, 0))
    return pl.pallas_call(
        functools.partial(_mm_residual_kernel, scale=scale),
        grid=(m // tm, n // tn),
        in_specs=[x_spec,
                  pl.BlockSpec((k, tn), lambda i, j: (0, j)),
                  pl.BlockSpec((tm, tn), lambda i, j: (i, j))],
        out_specs=pl.BlockSpec((tm, tn), lambda i, j: (i, j)),
        out_shape=jax.ShapeDtypeStruct((m, n), F32),
        compiler_params=_params(2),
        name="mm_residual",
    )(x, w, res)


def _rope_head(seg, cos, sin_lo, sin_hi):
    up = pltpu.roll(seg, HEAD_DIM - ROT_DIM // 2, axis=1)
    down = pltpu.roll(seg, ROT_DIM // 2, axis=1)
    return seg * cos + up * sin_lo + down * sin_hi


def _proj_rope_kernel(x_ref, w_ref, cos_ref, slo_ref, shi_ref, o_ref):
    acc = jnp.dot(x_ref[...], w_ref[...], preferred_element_type=F32)
    cos, slo, shi = cos_ref[...], slo_ref[...], shi_ref[...]
    for h in range(acc.shape[1] // HEAD_DIM):
        sl = slice(h * HEAD_DIM, (h + 1) * HEAD_DIM)
        o_ref[:, sl] = _rope_head(acc[:, sl], cos, slo, shi).astype(o_ref.dtype)


def _proj_plain_kernel(x_ref, w_ref, o_ref):
    o_ref[...] = jnp.dot(x_ref[...], w_ref[...], preferred_element_type=F32).astype(o_ref.dtype)


def _proj_gate_kernel(x_ref, w_ref, o_ref):
    acc = jnp.dot(x_ref[...], w_ref[...], preferred_element_type=F32)
    o_ref[...] = jax.nn.sigmoid(acc).astype(o_ref.dtype)


def _proj(xn, w, kind, tn, tables=(), tm=1024):
    m, k = xn.shape
    n = w.shape[1]
    body = {"rope": _proj_rope_kernel, "plain": _proj_plain_kernel, "gate": _proj_gate_kernel}[kind]
    tab = pl.BlockSpec((tm, HEAD_DIM), lambda i, j: (i, 0))
    return pl.pallas_call(
        body,
        grid=(m // tm, n // tn),
        in_specs=[pl.BlockSpec((tm, k), lambda i, j: (i, 0)),
                  pl.BlockSpec((k, tn), lambda i, j: (0, j))] + [tab] * len(tables),
        out_specs=pl.BlockSpec((tm, tn), lambda i, j: (i, j)),
        out_shape=jax.ShapeDtypeStruct((m, n), BF16),
        compiler_params=_params(2),
        name=f"proj_{kind}",
    )(xn, w, *tables)


def _seq_bounds(geo, g0):
    lo = jnp.int32(geo.starts[-1])
    hi = jnp.int32(geo.ends[-1])
    for s, e in zip(reversed(geo.starts[:-1]), reversed(geo.ends[:-1])):
        lo = jnp.where(g0 < e, s, lo)
        hi = jnp.where(g0 < e, e, hi)
    return lo, hi


def _band_mask(tq, nk, halo, half_window, lo, hi, u0):
    ii = lax.broadcasted_iota(jnp.int32, (tq, nk), 0)
    jj = lax.broadcasted_iota(jnp.int32, (tq, nk), 1)
    rel = jj - ii - halo
    return ((rel >= -half_window) & (rel <= half_window)
            & (jj >= lo - u0 + halo) & (jj < hi - u0 + halo))


A_TQ = 128


def _attn_a_kernel(q_ref, kp_ref, kc_ref, kn_ref, vp_ref, vc_ref, vn_ref, o_ref, l_ref, *, dil, geo):
    u0 = pl.program_id(0) * A_TQ
    plane_rows = geo.m // dil
    base = (u0 // plane_rows) * plane_rows
    lo, hi = _seq_bounds(geo, (u0 - base) * dil)
    nk = A_TQ + 2 * A_HALF
    valid = _band_mask(A_TQ, nk, A_HALF, A_HALF, base + lo // dil, base + hi // dil, u0)
    k = jnp.concatenate([kp_ref[...], kc_ref[...], kn_ref[...]], axis=0)
    v = jnp.concatenate([vp_ref[...], vc_ref[...], vn_ref[...]], axis=0)
    for h in range(A_HEADS):
        sl = slice(h * HEAD_DIM, (h + 1) * HEAD_DIM)
        s = lax.dot_general(q_ref[:, sl], k[:, sl], (((1,), (1,)), ((), ())),
                            preferred_element_type=F32) * SCALE
        s = jnp.where(valid, s, NEG_INF)
        m = jnp.max(s, axis=-1, keepdims=True)
        p = jnp.exp(s - m)
        den = jnp.sum(p, axis=-1, keepdims=True)
        o = jnp.dot(p.astype(BF16), v[:, sl], preferred_element_type=F32)
        o_ref[:, sl] = o * (1.0 / den)
        l_ref[:, sl] = jnp.broadcast_to(m + jnp.log(den), (A_TQ, HEAD_DIM))


def _attn_a(geo, dil, q_arr, q_blk, k_arr, k_blk, v_arr, v_blk):
    m = geo.m
    hb = A_TQ // A_HALF
    last_halo = m // A_HALF - 1

    def main(blk):
        return pl.BlockSpec((A_TQ, A_WIDTH), lambda t: (t, blk))

    def prev(blk):
        return pl.BlockSpec((A_HALF, A_WIDTH), lambda t: (jnp.maximum(t * hb - 1, 0), blk))

    def nxt(blk):
        return pl.BlockSpec((A_HALF, A_WIDTH), lambda t: (jnp.minimum(t * hb + hb, last_halo), blk))

    out_spec = pl.BlockSpec((A_TQ, A_WIDTH), lambda t: (t, 0))
    out_shape = jax.ShapeDtypeStruct((m, A_WIDTH), F32)
    return pl.pallas_call(
        functools.partial(_attn_a_kernel, dil=dil, geo=geo),
        grid=(m // A_TQ,),
        in_specs=[main(q_blk), prev(k_blk), main(k_blk), nxt(k_blk),
                  prev(v_blk), main(v_blk), nxt(v_blk)],
        out_specs=[out_spec, out_spec],
        out_shape=[out_shape, out_shape],
        compiler_params=_params(1),
        name=f"attn_a_d{dil}",
    )(q_arr, k_arr, k_arr, k_arr, v_arr, v_arr, v_arr)


COMBINE_TM = 256
LANE_SLABS = A_WIDTH // HEAD_DIM


def _combine_a_kernel(o0_ref, l0_ref, o1_ref, l1_ref, o2_ref, l2_ref, y_ref, o1_s, l1_s, o2_s, l2_s):
    for src, dst, dil in ((o1_ref, o1_s, DIL_GROUPS[1][1]), (l1_ref, l1_s, DIL_GROUPS[1][1]),
                          (o2_ref, o2_s, DIL_GROUPS[2][1]), (l2_ref, l2_s, DIL_GROUPS[2][1])):
        per = COMBINE_TM // dil
        for c in range(LANE_SLABS):
            for r in range(dil):
                dst.at[c][pl.ds(r, per, stride=dil), :] = src[r, :, c * HEAD_DIM:(c + 1) * HEAD_DIM]
    for c in range(LANE_SLABS):
        sl = slice(c * HEAD_DIM, (c + 1) * HEAD_DIM)
        a0, a1, a2 = l0_ref[:, sl], l1_s[c], l2_s[c]
        mx = jnp.maximum(jnp.maximum(a0, a1), a2)
        e0, e1, e2 = jnp.exp(a0 - mx), jnp.exp(a1 - mx), jnp.exp(a2 - mx)
        inv = 1.0 / (e0 + e1 + e2)
        y = (e0 * inv) * o0_ref[:, sl] + (e1 * inv) * o1_s[c] + (e2 * inv) * o2_s[c]
        y_ref[:, sl] = y.astype(y_ref.dtype)


def _combine_a(outs, lses):
    m = outs[0].shape[0]
    tm = COMBINE_TM
    d1, d2 = DIL_GROUPS[1][1], DIL_GROUPS[2][1]
    nat = pl.BlockSpec((tm, A_WIDTH), lambda i: (i, 0))
    p1 = pl.BlockSpec((d1, tm // d1, A_WIDTH), lambda i: (0, i, 0))
    p2 = pl.BlockSpec((d2, tm // d2, A_WIDTH), lambda i: (0, i, 0))
    slab = pltpu.VMEM((LANE_SLABS, tm, HEAD_DIM), F32)
    return pl.pallas_call(
        _combine_a_kernel,
        grid=(m // tm,),
        in_specs=[nat, nat, p1, p1, p2, p2],
        out_specs=nat,
        out_shape=jax.ShapeDtypeStruct((m, A_WIDTH), BF16),
        scratch_shapes=[slab, slab, slab, slab],
        compiler_params=_params(1),
        name="combine_a",
    )(outs[0], lses[0],
      outs[1].reshape(d1, m // d1, A_WIDTH), lses[1].reshape(d1, m // d1, A_WIDTH),
      outs[2].reshape(d2, m // d2, A_WIDTH), lses[2].reshape(d2, m // d2, A_WIDTH))


B_TQ = 128


def _attn_b_kernel(sink_ref, q_ref, kp_ref, kc_ref, kn_ref, vp_ref, vc_ref, vn_ref, o_ref, *, geo):
    u0 = pl.program_id(0) * B_TQ
    lo, hi = _seq_bounds(geo, u0)
    nk = 3 * B_TQ
    valid = _band_mask(B_TQ, nk, B_TQ, B_WINDOW, lo, hi, u0)
    k = jnp.concatenate([kp_ref[...], kc_ref[...], kn_ref[...]], axis=0)
    v = jnp.concatenate([vp_ref[...], vc_ref[...], vn_ref[...]], axis=0)
    for h in range(B_Q_HEADS):
        sl = slice(h * HEAD_DIM, (h + 1) * HEAD_DIM)
        g = h // B_GROUP
        gl = slice(g * HEAD_DIM, (g + 1) * HEAD_DIM)
        sk = sink_ref[h]
        s = lax.dot_general(q_ref[:, sl], k[:, gl], (((1,), (1,)), ((), ())),
                            preferred_element_type=F32) * SCALE
        s = jnp.where(valid, s, NEG_INF)
        m = jnp.maximum(jnp.max(s, axis=-1, keepdims=True), sk)
        p = jnp.exp(s - m)
        den = jnp.sum(p, axis=-1, keepdims=True) + jnp.exp(sk - m)
        o = jnp.dot(p.astype(BF16), v[:, gl], preferred_element_type=F32)
        o_ref[:, sl] = (o * (1.0 / den)).astype(o_ref.dtype)


def _attn_b(geo, sink, q_arr, q_blk, k_arr, k_blk, v_arr, v_blk):
    m = geo.m
    n_tiles = m // B_TQ

    def kv(blk, shift):
        return pl.BlockSpec((B_TQ, B_KV_WIDTH), lambda t: (jnp.clip(t + shift, 0, n_tiles - 1), blk))

    return pl.pallas_call(
        functools.partial(_attn_b_kernel, geo=geo),
        grid=(n_tiles,),
        in_specs=[pl.BlockSpec(memory_space=pltpu.SMEM),
                  pl.BlockSpec((B_TQ, B_Q_WIDTH), lambda t: (t, q_blk)),
                  kv(k_blk, -1), kv(k_blk, 0), kv(k_blk, 1),
                  kv(v_blk, -1), kv(v_blk, 0), kv(v_blk, 1)],
        out_specs=pl.BlockSpec((B_TQ, B_Q_WIDTH), lambda t: (t, 0)),
        out_shape=jax.ShapeDtypeStruct((m, B_Q_WIDTH), BF16),
        compiler_params=_params(1),
        name="attn_b",
    )(sink, q_arr, k_arr, k_arr, k_arr, v_arr, v_arr, v_arr)


def _merge_kernel(ya_ref, yb_ref, wa_ref, wb_ref, ga_ref, gb_ref, o_ref):
    a = jnp.dot(ya_ref[...], wa_ref[...], preferred_element_type=F32)
    b = jnp.dot(yb_ref[...], wb_ref[...], preferred_element_type=F32)
    o_ref[...] = (ga_ref[...].astype(F32) * a + gb_ref[...].astype(F32) * b).astype(o_ref.dtype)


def _merge(ya, yb, wa, wb, gates, tm=1024, tn=1024):
    m = ya.shape[0]
    n = wa.shape[1]
    return pl.pallas_call(
        _merge_kernel,
        grid=(m // tm, n // tn),
        in_specs=[pl.BlockSpec((tm, ya.shape[1]), lambda i, j: (i, 0)),
                  pl.BlockSpec((tm, yb.shape[1]), lambda i, j: (i, 0)),
                  pl.BlockSpec((wa.shape[0], tn), lambda i, j: (0, j)),
                  pl.BlockSpec((wb.shape[0], tn), lambda i, j: (0, j)),
                  pl.BlockSpec((tm, tn), lambda i, j: (i, j)),
                  pl.BlockSpec((tm, tn), lambda i, j: (i, n // tn + j))],
        out_specs=pl.BlockSpec((tm, tn), lambda i, j: (i, j)),
        out_shape=jax.ShapeDtypeStruct((m, n), BF16),
        compiler_params=_params(2),
        name="merge",
    )(ya, yb, wa, wb, gates, gates)


def _rope_tables(geo, dil):
    half = ROT_DIM // 2
    pos = jnp.concatenate([jnp.arange(e - s) for s, e in zip(geo.starts, geo.ends)])
    pos = pos.reshape(geo.m // dil, dil).T.reshape(-1)
    inv = ROPE_THETA ** (-jnp.arange(half, dtype=F32) / half)
    ang = pos.astype(F32)[:, None] * inv[None, :]
    cos, sin = jnp.cos(ang), jnp.sin(ang)
    rest = jnp.zeros((geo.m, HEAD_DIM - ROT_DIM), F32)
    zh = jnp.zeros((geo.m, half), F32)
    cos_t = jnp.concatenate([cos, cos, rest + 1.0], axis=1)
    sin_lo = jnp.concatenate([-sin, zh, rest], axis=1)
    sin_hi = jnp.concatenate([zh, sin, rest], axis=1)
    return cos_t, sin_lo, sin_hi


def _mixers(geo, h, g_mix, w_in, sink):
    n_nat, n_d4, n_d16 = _rmsnorm_planes(h, g_mix)
    sources = (n_nat, n_d4, n_d16)

    qa, ka, va = (w_in[:, i * 3 * A_WIDTH:(i + 1) * 3 * A_WIDTH] for i in range(3))
    off = 9 * A_WIDTH
    qb = w_in[:, off:off + B_Q_WIDTH]
    kb = w_in[:, off + B_Q_WIDTH:off + B_Q_WIDTH + B_KV_WIDTH]
    vb = w_in[:, off + B_Q_WIDTH + B_KV_WIDTH:off + B_Q_WIDTH + 2 * B_KV_WIDTH]
    wg = w_in[:, off + B_Q_WIDTH + 2 * B_KV_WIDTH:]

    def grp(w, gi):
        return w[:, gi * A_WIDTH:(gi + 1) * A_WIDTH]

    def cat(*ws):
        return jnp.concatenate(ws, axis=1).astype(BF16)

    rope0 = _proj(n_nat, cat(grp(qa, 0), grp(ka, 0), qb, kb), "rope", B_KV_WIDTH, _rope_tables(geo, 1))
    plain0 = _proj(n_nat, cat(grp(va, 0), vb), "plain", B_KV_WIDTH)
    gates = _proj(n_nat, wg.astype(BF16), "gate", 1024)

    outs, lses = [], []
    o, l = _attn_a(geo, 1, rope0, 0, rope0, 1, plain0, 0)
    outs.append(o)
    lses.append(l)
    for gi in (1, 2):
        dil = DIL_GROUPS[gi][1]
        rope = _proj(sources[gi], cat(grp(qa, gi), grp(ka, gi)), "rope", 1024, _rope_tables(geo, dil))
        plain = _proj(sources[gi], grp(va, gi).astype(BF16), "plain", 1024)
        o, l = _attn_a(geo, dil, rope, 0, rope, 1, plain, 0)
        outs.append(o)
        lses.append(l)
    ya = _combine_a(outs, lses)

    yb = _attn_b(geo, sink, rope0, 2 * A_WIDTH // B_Q_WIDTH, rope0, (2 * A_WIDTH + B_Q_WIDTH) // B_KV_WIDTH,
                 plain0, A_WIDTH // B_KV_WIDTH)
    return ya, yb, gates


def _ffn_half_step(x, g, w_gate, w_up, w_down):
    xn = _rmsnorm(x, g, BF16)
    act = _ffn_up(xn, w_gate.astype(BF16), w_up.astype(BF16))
    return _mm_residual(act, w_down.astype(BF16), x, 0.5, tm=1024, tn=256, single_buffer_x=True)


def _layer(geo, x, g_ffn1, w1_gate, w1_up, w1_down, g_mix, w_in, sink_b, w_branch_a, w_branch_b,
           w_out, g_ffn2, w2_gate, w2_up, w2_down, g_final):
    h = _ffn_half_step(x, g_ffn1, w1_gate, w1_up, w1_down)
    ya, yb, gates = _mixers(geo, h, g_mix, w_in, sink_b)
    merged = _merge(ya, yb, w_branch_a.astype(BF16), w_branch_b.astype(BF16), gates)
    h2 = _mm_residual(merged, w_out.astype(BF16), h, 1.0, tm=1024, tn=1024)
    x2 = _ffn_half_step(h2, g_ffn2, w2_gate, w2_up, w2_down)
    return _rmsnorm(x2, g_final, F32)


def kernel(x_prompt, x_sample, g_ffn1, w1_gate, w1_up, w1_down, g_mix, w_in, sink_b, w_branch_a, w_branch_b, w_out, g_ffn2, w2_gate, w2_up, w2_down, g_final):
    assert w1_gate.shape[0] == 1, "single layer"
    bp, sp, _ = x_prompt.shape
    bs, ss, _ = x_sample.shape
    lens = (sp,) * bp + (ss,) * bs
    ends = tuple(sum(lens[:i + 1]) for i in range(len(lens)))
    geo = Geometry(starts=(0,) + ends[:-1], ends=ends)
    x = jnp.concatenate([x_prompt.reshape(-1, D_MODEL), x_sample.reshape(-1, D_MODEL)], axis=0)

    y = _layer(geo, x, g_ffn1[0], w1_gate[0], w1_up[0], w1_down[0], g_mix[0], w_in[0], sink_b[0],
               w_branch_a[0], w_branch_b[0], w_out[0], g_ffn2[0], w2_gate[0], w2_up[0], w2_down[0], g_final)

    n_prompt = bp * sp
    return (y[:n_prompt].reshape(x_prompt.shape), y[n_prompt:].reshape(x_sample.shape))
```

```python
import functools
import math
from typing import NamedTuple

import jax
import jax.numpy as jnp
from jax import lax
from jax.experimental import pallas as pl
from jax.experimental.pallas import tpu as pltpu

F32 = jnp.float32
BF16 = jnp.bfloat16

D_MODEL = 4096
D_FF = 11008
HEAD_DIM = 128
ROT_DIM = HEAD_DIM // 4
ROPE_THETA = 500000.0
DIL_GROUPS = ((128, 1), (512, 4), (2048, 16))
A_HEADS = 8
A_WIDTH = A_HEADS * HEAD_DIM
A_HALF = 64
B_Q_HEADS = 16
B_KV_HEADS = 4
B_GROUP = B_Q_HEADS // B_KV_HEADS
B_Q_WIDTH = B_Q_HEADS * HEAD_DIM
B_KV_WIDTH = B_KV_HEADS * HEAD_DIM
B_WINDOW = 128
EPS = 1e-6
NEG_INF = -1e30
SCALE = 1.0 / math.sqrt(HEAD_DIM)
assert all((win // 2) // dil == A_HALF for win, dil in DIL_GROUPS)

VMEM_LIMIT = 58 * 1024 * 1024


class Geometry(NamedTuple):
    starts: tuple
    ends: tuple

    @property
    def m(self):
        return self.ends[-1]


def _params(n_grid):
    return pltpu.CompilerParams(
        dimension_semantics=("arbitrary",) * n_grid,
        vmem_limit_bytes=VMEM_LIMIT)


def _normalise(x, g):
    var = jnp.mean(x * x, axis=-1, keepdims=True)
    return (x * lax.rsqrt(var + EPS)) * g


def _rmsnorm_kernel(x_ref, g_ref, o_ref):
    o_ref[...] = _normalise(x_ref[...], g_ref[...]).astype(o_ref.dtype)


def _rmsnorm(x, g, out_dtype, row0=0, rows=None, tm=256):
    d = x.shape[1]
    rows = x.shape[0] if rows is None else rows
    blk0 = row0 // tm
    assert row0 % tm == 0 and rows % tm == 0
    return pl.pallas_call(
        _rmsnorm_kernel,
        grid=(rows // tm,),
        in_specs=[pl.BlockSpec((tm, d), lambda i: (i + blk0, 0)),
                  pl.BlockSpec((1, d), lambda i: (0, 0))],
        out_specs=pl.BlockSpec((tm, d), lambda i: (i, 0)),
        out_shape=jax.ShapeDtypeStruct((rows, d), out_dtype),
        compiler_params=_params(1),
        name="rmsnorm",
    )(x, g.reshape(1, d))


NORM_TM = 256


def _plane_permutation(tm, dil):
    per = tm // dil
    assert per & (per - 1) == 0, "rows per plane must be a power of two"
    i = lax.broadcasted_iota(jnp.int32, (tm, tm), 0)
    j = lax.broadcasted_iota(jnp.int32, (tm, tm), 1)
    src = (i & (per - 1)) * dil + (i >> (per.bit_length() - 1))
    return (j == src).astype(BF16)


def _rmsnorm_planes_kernel(x_ref, g_ref, n_ref, n4_ref, n16_ref):
    y = _normalise(x_ref[...], g_ref[...]).astype(BF16)
    n_ref[...] = y
    for ref, dil in ((n4_ref, DIL_GROUPS[1][1]), (n16_ref, DIL_GROUPS[2][1])):
        perm = jnp.dot(_plane_permutation(NORM_TM, dil), y, preferred_element_type=F32).astype(BF16)
        ref[...] = perm.reshape(dil, NORM_TM // dil, y.shape[1])


def _rmsnorm_planes(x, g):
    m, d = x.shape
    tm = NORM_TM
    d4, d16 = DIL_GROUPS[1][1], DIL_GROUPS[2][1]
    n, n4, n16 = pl.pallas_call(
        _rmsnorm_planes_kernel,
        grid=(m // tm,),
        in_specs=[pl.BlockSpec((tm, d), lambda i: (i, 0)),
                  pl.BlockSpec((1, d), lambda i: (0, 0))],
        out_specs=[pl.BlockSpec((tm, d), lambda i: (i, 0)),
                   pl.BlockSpec((d4, tm // d4, d), lambda i: (0, i, 0)),
                   pl.BlockSpec((d16, tm // d16, d), lambda i: (0, i, 0))],
        out_shape=[jax.ShapeDtypeStruct((m, d), BF16),
                   jax.ShapeDtypeStruct((d4, m // d4, d), BF16),
                   jax.ShapeDtypeStruct((d16, m // d16, d), BF16)],
        compiler_params=_params(1),
        name="rmsnorm_planes",
    )(x, g.reshape(1, d))
    return n, n4.reshape(m, d), n16.reshape(m, d)


def _ffn_up_kernel(x_ref, wg_ref, wu_ref, *rest):
    o_ref = rest[-1]
    x = x_ref[...]
    g = jnp.dot(x, wg_ref[...], preferred_element_type=F32)
    u = jnp.dot(x, wu_ref[...], preferred_element_type=F32)
    o_ref[...] = (g * jax.nn.sigmoid(g) * u).astype(o_ref.dtype)


FFN_TM = 1024
FFN_TN = 512
FFN_TAIL = D_FF % FFN_TN
assert FFN_TAIL % HEAD_DIM == 0 and (D_FF - FFN_TAIL) % FFN_TAIL == 0


def _ffn_up(xn, wg, wu):
    m, k = xn.shape
    n = wg.shape[1]
    tm = FFN_TM
    x_spec = pl.BlockSpec((tm, k), lambda i, j: (i, 0))
    act = pl.pallas_call(
        _ffn_up_kernel,
        grid=(m // tm, n // FFN_TN),
        in_specs=[x_spec,
                  pl.BlockSpec((k, FFN_TN), lambda i, j: (0, j)),
                  pl.BlockSpec((k, FFN_TN), lambda i, j: (0, j))],
        out_specs=pl.BlockSpec((tm, FFN_TN), lambda i, j: (i, j)),
        out_shape=jax.ShapeDtypeStruct((m, n), BF16),
        compiler_params=_params(2),
        name="ffn_up",
    )(xn, wg, wu)
    tail_block = (n - FFN_TAIL) // FFN_TAIL
    w_tail = pl.BlockSpec((k, FFN_TAIL), lambda i, j: (0, tail_block))
    return pl.pallas_call(
        _ffn_up_kernel,
        grid=(m // tm, 1),
        in_specs=[x_spec, w_tail, w_tail, pl.BlockSpec(memory_space=pl.ANY)],
        out_specs=pl.BlockSpec((tm, FFN_TAIL), lambda i, j: (i, tail_block)),
        out_shape=jax.ShapeDtypeStruct((m, n), BF16),
        input_output_aliases={3: 0},
        compiler_params=_params(2),
        name="ffn_up_tail",
    )(xn, wg, wu, act)


def _mm_residual_kernel(x_ref, w_ref, r_ref, *rest, scale):
    o_ref = rest[-1]
    acc = jnp.dot(x_ref[...], w_ref[...], preferred_element_type=F32)
    o_ref[...] = r_ref[...] + scale * acc


def _mm_residual(x, w, res, scale, tm, tn, k_splits=1, out_rows=None, out_row0=0, dst=None):
    m, k = x.shape
    n = w.shape[1]
    out_rows = m if out_rows is None else out_rows
    blk0 = out_row0 // tm
    tk = k // k_splits
    assert out_row0 % tm == 0 and k % k_splits == 0 and tk % HEAD_DIM == 0
    for s in range(k_splits):
        first = s == 0
        args = [x, w, res if first else dst]
        in_specs = [pl.BlockSpec((tm, tk), lambda i, j, s=s: (i, s)),
                    pl.BlockSpec((tk, tn), lambda i, j, s=s: (s, j)),
                    pl.BlockSpec((tm, tn), (lambda i, j: (i, j)) if first else (lambda i, j: (i + blk0, j)))]
        aliases = {}
        if first and dst is not None:
            args.append(dst)
            in_specs.append(pl.BlockSpec(memory_space=pl.ANY))
            aliases = {3: 0}
        elif not first:
            aliases = {2: 0}
        dst = pl.pallas_call(
            functools.partial(_mm_residual_kernel, scale=scale),
            grid=(m // tm, n // tn),
            in_specs=in_specs,
            out_specs=pl.BlockSpec((tm, tn), lambda i, j: (i + blk0, j)),
            out_shape=jax.ShapeDtypeStruct((out_rows, n), F32),
            input_output_aliases=aliases,
            compiler_params=_params(2),
            name="mm_residual",
        )(*args)
    return dst


def _rope_head(seg, cos, sin_lo, sin_hi):
    up = pltpu.roll(seg, HEAD_DIM - ROT_DIM // 2, axis=1)
    down = pltpu.roll(seg, ROT_DIM // 2, axis=1)
    return seg * cos + up * sin_lo + down * sin_hi


def _proj_rope_kernel(x_ref, w_ref, cos_ref, slo_ref, shi_ref, o_ref):
    acc = jnp.dot(x_ref[...], w_ref[...], preferred_element_type=F32)
    cos, slo, shi = cos_ref[...], slo_ref[...], shi_ref[...]
    for h in range(acc.shape[1] // HEAD_DIM):
        sl = slice(h * HEAD_DIM, (h + 1) * HEAD_DIM)
        o_ref[:, sl] = _rope_head(acc[:, sl], cos, slo, shi).astype(o_ref.dtype)


def _proj_plain_kernel(x_ref, w_ref, o_ref):
    o_ref[...] = jnp.dot(x_ref[...], w_ref[...], preferred_element_type=F32).astype(o_ref.dtype)


def _proj_gate_kernel(x_ref, w_ref, o_ref):
    acc = jnp.dot(x_ref[...], w_ref[...], preferred_element_type=F32)
    o_ref[...] = jax.nn.sigmoid(acc).astype(o_ref.dtype)


def _proj(xn, w, kind, tn, tables=(), tm=1024):
    m, k = xn.shape
    n = w.shape[1]
    body = {"rope": _proj_rope_kernel, "plain": _proj_plain_kernel, "gate": _proj_gate_kernel}[kind]
    tab = pl.BlockSpec((tm, HEAD_DIM), lambda i, j: (i, 0))
    return pl.pallas_call(
        body,
        grid=(m // tm, n // tn),
        in_specs=[pl.BlockSpec((tm, k), lambda i, j: (i, 0)),
                  pl.BlockSpec((k, tn), lambda i, j: (0, j))] + [tab] * len(tables),
        out_specs=pl.BlockSpec((tm, tn), lambda i, j: (i, j)),
        out_shape=jax.ShapeDtypeStruct((m, n), BF16),
        compiler_params=_params(2),
        name=f"proj_{kind}",
    )(xn, w, *tables)


def _seq_bounds(geo, g0):
    lo = jnp.int32(geo.starts[-1])
    hi = jnp.int32(geo.ends[-1])
    for s, e in zip(reversed(geo.starts[:-1]), reversed(geo.ends[:-1])):
        lo = jnp.where(g0 < e, s, lo)
        hi = jnp.where(g0 < e, e, hi)
    return lo, hi


def _band_mask(tq, nk, halo, half_window, lo, hi, u0, stacked=1):
    assert tq & (tq - 1) == 0
    ii = lax.broadcasted_iota(jnp.int32, (stacked * tq, nk), 0) & (tq - 1)
    jj = lax.broadcasted_iota(jnp.int32, (stacked * tq, nk), 1)
    rel = jj - ii - halo
    return ((rel >= -half_window) & (rel <= half_window)
            & (jj >= lo - u0 + halo) & (jj < hi - u0 + halo))


A_TQ = 128


def _attn_a_kernel(q_ref, kp_ref, kc_ref, kn_ref, vp_ref, vc_ref, vn_ref, o_ref, l_ref, *, dil, geo):
    u0 = pl.program_id(0) * A_TQ
    plane_rows = geo.m // dil
    base = (u0 // plane_rows) * plane_rows
    lo, hi = _seq_bounds(geo, (u0 - base) * dil)
    nk = A_TQ + 2 * A_HALF
    valid = _band_mask(A_TQ, nk, A_HALF, A_HALF, base + lo // dil, base + hi // dil, u0)
    k = jnp.concatenate([kp_ref[...], kc_ref[...], kn_ref[...]], axis=0)
    v = jnp.concatenate([vp_ref[...], vc_ref[...], vn_ref[...]], axis=0)
    for h in range(A_HEADS):
        sl = slice(h * HEAD_DIM, (h + 1) * HEAD_DIM)
        s = lax.dot_general(q_ref[:, sl], k[:, sl], (((1,), (1,)), ((), ())),
                            preferred_element_type=F32) * SCALE
        s = jnp.where(valid, s, NEG_INF)
        m = jnp.max(s, axis=-1, keepdims=True)
        p = jnp.exp(s - m)
        den = jnp.sum(p, axis=-1, keepdims=True)
        o = jnp.dot(p.astype(BF16), v[:, sl], preferred_element_type=F32)
        o_ref[:, sl] = o * (1.0 / den)
        l_ref[:, sl] = jnp.broadcast_to(m + jnp.log(den), (A_TQ, HEAD_DIM))


def _attn_a(geo, dil, q_arr, q_blk, k_arr, k_blk, v_arr, v_blk):
    m = geo.m
    hb = A_TQ // A_HALF
    last_halo = m // A_HALF - 1

    def main(blk):
        return pl.BlockSpec((A_TQ, A_WIDTH), lambda t: (t, blk))

    def prev(blk):
        return pl.BlockSpec((A_HALF, A_WIDTH), lambda t: (jnp.maximum(t * hb - 1, 0), blk))

    def nxt(blk):
        return pl.BlockSpec((A_HALF, A_WIDTH), lambda t: (jnp.minimum(t * hb + hb, last_halo), blk))

    out_spec = pl.BlockSpec((A_TQ, A_WIDTH), lambda t: (t, 0))
    out_shape = jax.ShapeDtypeStruct((m, A_WIDTH), F32)
    return pl.pallas_call(
        functools.partial(_attn_a_kernel, dil=dil, geo=geo),
        grid=(m // A_TQ,),
        in_specs=[main(q_blk), prev(k_blk), main(k_blk), nxt(k_blk),
                  prev(v_blk), main(v_blk), nxt(v_blk)],
        out_specs=[out_spec, out_spec],
        out_shape=[out_shape, out_shape],
        compiler_params=_params(1),
        name=f"attn_a_d{dil}",
    )(q_arr, k_arr, k_arr, k_arr, v_arr, v_arr, v_arr)


COMBINE_TM = 256
LANE_SLABS = A_WIDTH // HEAD_DIM


def _combine_a_kernel(o0_ref, l0_ref, o1_ref, l1_ref, o2_ref, l2_ref, y_ref, o1_s, l1_s, o2_s, l2_s):
    for src, dst, dil in ((o1_ref, o1_s, DIL_GROUPS[1][1]), (l1_ref, l1_s, DIL_GROUPS[1][1]),
                          (o2_ref, o2_s, DIL_GROUPS[2][1]), (l2_ref, l2_s, DIL_GROUPS[2][1])):
        per = COMBINE_TM // dil
        for c in range(LANE_SLABS):
            for r in range(dil):
                dst.at[c][pl.ds(r, per, stride=dil), :] = src[r, :, c * HEAD_DIM:(c + 1) * HEAD_DIM]
    for c in range(LANE_SLABS):
        sl = slice(c * HEAD_DIM, (c + 1) * HEAD_DIM)
        a0, a1, a2 = l0_ref[:, sl], l1_s[c], l2_s[c]
        mx = jnp.maximum(jnp.maximum(a0, a1), a2)
        e0, e1, e2 = jnp.exp(a0 - mx), jnp.exp(a1 - mx), jnp.exp(a2 - mx)
        inv = 1.0 / (e0 + e1 + e2)
        y = (e0 * inv) * o0_ref[:, sl] + (e1 * inv) * o1_s[c] + (e2 * inv) * o2_s[c]
        y_ref[:, sl] = y.astype(y_ref.dtype)


def _combine_a(outs, lses):
    m = outs[0].shape[0]
    tm = COMBINE_TM
    d1, d2 = DIL_GROUPS[1][1], DIL_GROUPS[2][1]
    nat = pl.BlockSpec((tm, A_WIDTH), lambda i: (i, 0))
    p1 = pl.BlockSpec((d1, tm // d1, A_WIDTH), lambda i: (0, i, 0))
    p2 = pl.BlockSpec((d2, tm // d2, A_WIDTH), lambda i: (0, i, 0))
    slab = pltpu.VMEM((LANE_SLABS, tm, HEAD_DIM), F32)
    return pl.pallas_call(
        _combine_a_kernel,
        grid=(m // tm,),
        in_specs=[nat, nat, p1, p1, p2, p2],
        out_specs=nat,
        out_shape=jax.ShapeDtypeStruct((m, A_WIDTH), BF16),
        scratch_shapes=[slab, slab, slab, slab],
        compiler_params=_params(1),
        name="combine_a",
    )(outs[0], lses[0],
      outs[1].reshape(d1, m // d1, A_WIDTH), lses[1].reshape(d1, m // d1, A_WIDTH),
      outs[2].reshape(d2, m // d2, A_WIDTH), lses[2].reshape(d2, m // d2, A_WIDTH))


B_TQ = 128


def _attn_b_kernel(sink_ref, q_ref, kp_ref, kc_ref, kn_ref, vp_ref, vc_ref, vn_ref, o_ref, *, geo):
    u0 = pl.program_id(0) * B_TQ
    lo, hi = _seq_bounds(geo, u0)
    nk = 3 * B_TQ
    valid = _band_mask(B_TQ, nk, B_TQ, B_WINDOW, lo, hi, u0, stacked=B_GROUP)
    k = jnp.concatenate([kp_ref[...], kc_ref[...], kn_ref[...]], axis=0)
    v = jnp.concatenate([vp_ref[...], vc_ref[...], vn_ref[...]], axis=0)
    row_head = lax.broadcasted_iota(jnp.int32, (B_GROUP * B_TQ, 1), 0) >> (B_TQ.bit_length() - 1)
    for g in range(B_KV_HEADS):
        heads = [slice((g * B_GROUP + h) * HEAD_DIM, (g * B_GROUP + h + 1) * HEAD_DIM) for h in range(B_GROUP)]
        gl = slice(g * HEAD_DIM, (g + 1) * HEAD_DIM)
        q = jnp.concatenate([q_ref[:, sl] for sl in heads], axis=0)
        sk = jnp.zeros((B_GROUP * B_TQ, 1), F32)
        for h in range(B_GROUP):
            sk = jnp.where(row_head == h, sink_ref[g * B_GROUP + h], sk)
        s = lax.dot_general(q, k[:, gl], (((1,), (1,)), ((), ())), preferred_element_type=F32) * SCALE
        s = jnp.where(valid, s, NEG_INF)
        m = jnp.maximum(jnp.max(s, axis=-1, keepdims=True), sk)
        p = jnp.exp(s - m)
        den = jnp.sum(p, axis=-1, keepdims=True) + jnp.exp(sk - m)
        o = jnp.dot(p.astype(BF16), v[:, gl], preferred_element_type=F32) * (1.0 / den)
        for h, sl in enumerate(heads):
            o_ref[:, sl] = o[h * B_TQ:(h + 1) * B_TQ].astype(o_ref.dtype)


def _attn_b(geo, sink, q_arr, q_blk, k_arr, k_blk, v_arr, v_blk):
    m = geo.m
    n_tiles = m // B_TQ

    def kv(blk, shift):
        return pl.BlockSpec((B_TQ, B_KV_WIDTH), lambda t: (jnp.clip(t + shift, 0, n_tiles - 1), blk))

    return pl.pallas_call(
        functools.partial(_attn_b_kernel, geo=geo),
        grid=(n_tiles,),
        in_specs=[pl.BlockSpec(memory_space=pltpu.SMEM),
                  pl.BlockSpec((B_TQ, B_Q_WIDTH), lambda t: (t, q_blk)),
                  kv(k_blk, -1), kv(k_blk, 0), kv(k_blk, 1),
                  kv(v_blk, -1), kv(v_blk, 0), kv(v_blk, 1)],
        out_specs=pl.BlockSpec((B_TQ, B_Q_WIDTH), lambda t: (t, 0)),
        out_shape=jax.ShapeDtypeStruct((m, B_Q_WIDTH), BF16),
        compiler_params=_params(1),
        name="attn_b",
    )(sink, q_arr, k_arr, k_arr, k_arr, v_arr, v_arr, v_arr)


def _merge_kernel(ya_ref, yb_ref, wa_ref, wb_ref, ga_ref, gb_ref, o_ref):
    a = jnp.dot(ya_ref[...], wa_ref[...], preferred_element_type=F32)
    b = jnp.dot(yb_ref[...], wb_ref[...], preferred_element_type=F32)
    o_ref[...] = (ga_ref[...].astype(F32) * a + gb_ref[...].astype(F32) * b).astype(o_ref.dtype)


def _merge(ya, yb, wa, wb, gates, tm=1024, tn=1024):
    m = ya.shape[0]
    n = wa.shape[1]
    return pl.pallas_call(
        _merge_kernel,
        grid=(m // tm, n // tn),
        in_specs=[pl.BlockSpec((tm, ya.shape[1]), lambda i, j: (i, 0)),
                  pl.BlockSpec((tm, yb.shape[1]), lambda i, j: (i, 0)),
                  pl.BlockSpec((wa.shape[0], tn), lambda i, j: (0, j)),
                  pl.BlockSpec((wb.shape[0], tn), lambda i, j: (0, j)),
                  pl.BlockSpec((tm, tn), lambda i, j: (i, j)),
                  pl.BlockSpec((tm, tn), lambda i, j: (i, n // tn + j))],
        out_specs=pl.BlockSpec((tm, tn), lambda i, j: (i, j)),
        out_shape=jax.ShapeDtypeStruct((m, n), BF16),
        compiler_params=_params(2),
        name="merge",
    )(ya, yb, wa, wb, gates, gates)


def _rope_tables(geo, dil):
    half = ROT_DIM // 2
    pos = jnp.concatenate([jnp.arange(e - s) for s, e in zip(geo.starts, geo.ends)])
    pos = pos.reshape(geo.m // dil, dil).T.reshape(-1)
    inv = ROPE_THETA ** (-jnp.arange(half, dtype=F32) / half)
    ang = pos.astype(F32)[:, None] * inv[None, :]
    cos, sin = jnp.cos(ang), jnp.sin(ang)
    rest = jnp.zeros((geo.m, HEAD_DIM - ROT_DIM), F32)
    zh = jnp.zeros((geo.m, half), F32)
    cos_t = jnp.concatenate([cos, cos, rest + 1.0], axis=1)
    sin_lo = jnp.concatenate([-sin, zh, rest], axis=1)
    sin_hi = jnp.concatenate([zh, sin, rest], axis=1)
    return cos_t, sin_lo, sin_hi


def _mixers(geo, h, g_mix, w_in, sink):
    n_nat, n_d4, n_d16 = _rmsnorm_planes(h, g_mix)
    sources = (n_nat, n_d4, n_d16)

    qa, ka, va = (w_in[:, i * 3 * A_WIDTH:(i + 1) * 3 * A_WIDTH] for i in range(3))
    off = 9 * A_WIDTH
    qb = w_in[:, off:off + B_Q_WIDTH]
    kb = w_in[:, off + B_Q_WIDTH:off + B_Q_WIDTH + B_KV_WIDTH]
    vb = w_in[:, off + B_Q_WIDTH + B_KV_WIDTH:off + B_Q_WIDTH + 2 * B_KV_WIDTH]
    wg = w_in[:, off + B_Q_WIDTH + 2 * B_KV_WIDTH:]

    def grp(w, gi):
        return w[:, gi * A_WIDTH:(gi + 1) * A_WIDTH]

    def cat(*ws):
        return jnp.concatenate(ws, axis=1).astype(BF16)

    rope0 = _proj(n_nat, cat(grp(qa, 0), grp(ka, 0), qb, kb), "rope", B_KV_WIDTH, _rope_tables(geo, 1))
    plain0 = _proj(n_nat, cat(grp(va, 0), vb), "plain", B_KV_WIDTH)
    gates = _proj(n_nat, wg.astype(BF16), "gate", 1024)

    outs, lses = [], []
    o, l = _attn_a(geo, 1, rope0, 0, rope0, 1, plain0, 0)
    outs.append(o)
    lses.append(l)
    for gi in (1, 2):
        dil = DIL_GROUPS[gi][1]
        rope = _proj(sources[gi], cat(grp(qa, gi), grp(ka, gi)), "rope", 1024, _rope_tables(geo, dil))
        plain = _proj(sources[gi], grp(va, gi).astype(BF16), "plain", 1024)
        o, l = _attn_a(geo, dil, rope, 0, rope, 1, plain, 0)
        outs.append(o)
        lses.append(l)
    ya = _combine_a(outs, lses)

    yb = _attn_b(geo, sink, rope0, 2 * A_WIDTH // B_Q_WIDTH, rope0, (2 * A_WIDTH + B_Q_WIDTH) // B_KV_WIDTH,
                 plain0, A_WIDTH // B_KV_WIDTH)
    return ya, yb, gates


def _ffn_half_step(parts, total_rows, g, w_gate, w_up, w_down):
    wg, wu, wd = w_gate.astype(BF16), w_up.astype(BF16), w_down.astype(BF16)
    out, row0 = None, 0
    for x in parts:
        act = _ffn_up(_rmsnorm(x, g, BF16), wg, wu)
        out = _mm_residual(act, wd, x, 0.5, tm=1024, tn=512, k_splits=2,
                           out_rows=total_rows, out_row0=row0, dst=out)
        row0 += x.shape[0]
    return out


def _layer(geo, parts, g_ffn1, w1_gate, w1_up, w1_down, g_mix, w_in, sink_b, w_branch_a, w_branch_b,
           w_out, g_ffn2, w2_gate, w2_up, w2_down, g_final):
    h = _ffn_half_step(parts, geo.m, g_ffn1, w1_gate, w1_up, w1_down)
    ya, yb, gates = _mixers(geo, h, g_mix, w_in, sink_b)
    merged = _merge(ya, yb, w_branch_a.astype(BF16), w_branch_b.astype(BF16), gates)
    h2 = _mm_residual(merged, w_out.astype(BF16), h, 1.0, tm=1024, tn=1024)
    x2 = _ffn_half_step([h2], geo.m, g_ffn2, w2_gate, w2_up, w2_down)
    outs, row0 = [], 0
    for x in parts:
        outs.append(_rmsnorm(x2, g_final, F32, row0=row0, rows=x.shape[0]))
        row0 += x.shape[0]
    return outs


def kernel(x_prompt, x_sample, g_ffn1, w1_gate, w1_up, w1_down, g_mix, w_in, sink_b, w_branch_a, w_branch_b, w_out, g_ffn2, w2_gate, w2_up, w2_down, g_final):
    assert w1_gate.shape[0] == 1, "single layer"
    bp, sp, _ = x_prompt.shape
    bs, ss, _ = x_sample.shape
    lens = (sp,) * bp + (ss,) * bs
    ends = tuple(sum(lens[:i + 1]) for i in range(len(lens)))
    geo = Geometry(starts=(0,) + ends[:-1], ends=ends)
    parts = [x_prompt.reshape(-1, D_MODEL), x_sample.reshape(-1, D_MODEL)]
    y_prompt, y_sample = _layer(
        geo, parts, g_ffn1[0], w1_gate[0], w1_up[0], w1_down[0], g_mix[0], w_in[0], sink_b[0],
        w_branch_a[0], w_branch_b[0], w_out[0], g_ffn2[0], w2_gate[0], w2_up[0], w2_down[0], g_final)
    return (y_prompt.reshape(x_prompt.shape), y_sample.reshape(x_sample.shape))
```

```python
import functools
import math
from typing import NamedTuple

import jax
import jax.numpy as jnp
from jax import lax
from jax.experimental import pallas as pl
from jax.experimental.pallas import tpu as pltpu

F32 = jnp.float32
BF16 = jnp.bfloat16

D_MODEL = 4096
D_FF = 11008
HEAD_DIM = 128
ROT_DIM = HEAD_DIM // 4
ROPE_THETA = 500000.0
DIL_GROUPS = ((128, 1), (512, 4), (2048, 16))
A_HEADS = 8
A_WIDTH = A_HEADS * HEAD_DIM
A_HALF = 64
B_Q_HEADS = 16
B_KV_HEADS = 4
B_GROUP = B_Q_HEADS // B_KV_HEADS
B_Q_WIDTH = B_Q_HEADS * HEAD_DIM
B_KV_WIDTH = B_KV_HEADS * HEAD_DIM
B_WINDOW = 128
EPS = 1e-6
NEG_INF = -1e30
SCALE = 1.0 / math.sqrt(HEAD_DIM)
assert all((win // 2) // dil == A_HALF for win, dil in DIL_GROUPS)

VMEM_LIMIT = 58 * 1024 * 1024


class Geometry(NamedTuple):
    starts: tuple
    ends: tuple

    @property
    def m(self):
        return self.ends[-1]


def _params(n_grid):
    return pltpu.CompilerParams(
        dimension_semantics=("arbitrary",) * n_grid,
        vmem_limit_bytes=VMEM_LIMIT)


def _normalise(x, g):
    var = jnp.mean(x * x, axis=-1, keepdims=True)
    return (x * lax.rsqrt(var + EPS)) * g


def _rmsnorm_kernel(x_ref, g_ref, o_ref):
    o_ref[...] = _normalise(x_ref[...], g_ref[...]).astype(o_ref.dtype)


def _rmsnorm(x, g, out_dtype, row0=0, rows=None, tm=256):
    d = x.shape[1]
    rows = x.shape[0] if rows is None else rows
    blk0 = row0 // tm
    assert row0 % tm == 0 and rows % tm == 0
    return pl.pallas_call(
        _rmsnorm_kernel,
        grid=(rows // tm,),
        in_specs=[pl.BlockSpec((tm, d), lambda i: (i + blk0, 0)),
                  pl.BlockSpec((1, d), lambda i: (0, 0))],
        out_specs=pl.BlockSpec((tm, d), lambda i: (i, 0)),
        out_shape=jax.ShapeDtypeStruct((rows, d), out_dtype),
        compiler_params=_params(1),
        name="rmsnorm",
    )(x, g.reshape(1, d))


NORM_TM = 256


def _plane_permutation(tm, dil):
    per = tm // dil
    assert per & (per - 1) == 0, "rows per plane must be a power of two"
    i = lax.broadcasted_iota(jnp.int32, (tm, tm), 0)
    j = lax.broadcasted_iota(jnp.int32, (tm, tm), 1)
    src = (i & (per - 1)) * dil + (i >> (per.bit_length() - 1))
    return (j == src).astype(BF16)


def _rmsnorm_planes_kernel(x_ref, g_ref, n_ref, n4_ref, n16_ref):
    y = _normalise(x_ref[...], g_ref[...]).astype(BF16)
    n_ref[...] = y
    for ref, dil in ((n4_ref, DIL_GROUPS[1][1]), (n16_ref, DIL_GROUPS[2][1])):
        perm = jnp.dot(_plane_permutation(NORM_TM, dil), y, preferred_element_type=F32).astype(BF16)
        ref[...] = perm.reshape(dil, NORM_TM // dil, y.shape[1])


def _rmsnorm_planes(x, g):
    m, d = x.shape
    tm = NORM_TM
    d4, d16 = DIL_GROUPS[1][1], DIL_GROUPS[2][1]
    n, n4, n16 = pl.pallas_call(
        _rmsnorm_planes_kernel,
        grid=(m // tm,),
        in_specs=[pl.BlockSpec((tm, d), lambda i: (i, 0)),
                  pl.BlockSpec((1, d), lambda i: (0, 0))],
        out_specs=[pl.BlockSpec((tm, d), lambda i: (i, 0)),
                   pl.BlockSpec((d4, tm // d4, d), lambda i: (0, i, 0)),
                   pl.BlockSpec((d16, tm // d16, d), lambda i: (0, i, 0))],
        out_shape=[jax.ShapeDtypeStruct((m, d), BF16),
                   jax.ShapeDtypeStruct((d4, m // d4, d), BF16),
                   jax.ShapeDtypeStruct((d16, m // d16, d), BF16)],
        compiler_params=_params(1),
        name="rmsnorm_planes",
    )(x, g.reshape(1, d))
    return n, n4.reshape(m, d), n16.reshape(m, d)


def _sigmoid(x):
    return 0.5 * jnp.tanh(0.5 * x) + 0.5


def _ffn_up_kernel(x_ref, wg_ref, wu_ref, *rest):
    o_ref = rest[-1]
    x = x_ref[...]
    g = jnp.dot(x, wg_ref[...], preferred_element_type=F32)
    u = jnp.dot(x, wu_ref[...], preferred_element_type=F32)
    o_ref[...] = (g * _sigmoid(g) * u).astype(o_ref.dtype)


FFN_TM = 1024
FFN_TN = 512
FFN_TAIL = D_FF % FFN_TN
assert FFN_TAIL % HEAD_DIM == 0 and (D_FF - FFN_TAIL) % FFN_TAIL == 0


def _ffn_up(xn, wg, wu):
    m, k = xn.shape
    n = wg.shape[1]
    tm = FFN_TM
    x_spec = pl.BlockSpec((tm, k), lambda i, j: (i, 0))
    act = pl.pallas_call(
        _ffn_up_kernel,
        grid=(m // tm, n // FFN_TN),
        in_specs=[x_spec,
                  pl.BlockSpec((k, FFN_TN), lambda i, j: (0, j)),
                  pl.BlockSpec((k, FFN_TN), lambda i, j: (0, j))],
        out_specs=pl.BlockSpec((tm, FFN_TN), lambda i, j: (i, j)),
        out_shape=jax.ShapeDtypeStruct((m, n), BF16),
        compiler_params=_params(2),
        name="ffn_up",
    )(xn, wg, wu)
    tail_block = (n - FFN_TAIL) // FFN_TAIL
    w_tail = pl.BlockSpec((k, FFN_TAIL), lambda i, j: (0, tail_block))
    return pl.pallas_call(
        _ffn_up_kernel,
        grid=(m // tm, 1),
        in_specs=[x_spec, w_tail, w_tail, pl.BlockSpec(memory_space=pl.ANY)],
        out_specs=pl.BlockSpec((tm, FFN_TAIL), lambda i, j: (i, tail_block)),
        out_shape=jax.ShapeDtypeStruct((m, n), BF16),
        input_output_aliases={3: 0},
        compiler_params=_params(2),
        name="ffn_up_tail",
    )(xn, wg, wu, act)


def _mm_residual_kernel(x_ref, w_ref, r_ref, *rest, scale):
    o_ref = rest[-1]
    acc = jnp.dot(x_ref[...], w_ref[...], preferred_element_type=F32)
    o_ref[...] = r_ref[...] + scale * acc


def _mm_residual(x, w, res, scale, tm, tn, k_splits=1, out_rows=None, out_row0=0, dst=None):
    m, k = x.shape
    n = w.shape[1]
    out_rows = m if out_rows is None else out_rows
    blk0 = out_row0 // tm
    tk = k // k_splits
    assert out_row0 % tm == 0 and k % k_splits == 0 and tk % HEAD_DIM == 0
    for s in range(k_splits):
        first = s == 0
        args = [x, w, res if first else dst]
        in_specs = [pl.BlockSpec((tm, tk), lambda i, j, s=s: (i, s)),
                    pl.BlockSpec((tk, tn), lambda i, j, s=s: (s, j)),
                    pl.BlockSpec((tm, tn), (lambda i, j: (i, j)) if first else (lambda i, j: (i + blk0, j)))]
        aliases = {}
        if first and dst is not None:
            args.append(dst)
            in_specs.append(pl.BlockSpec(memory_space=pl.ANY))
            aliases = {3: 0}
        elif not first:
            aliases = {2: 0}
        dst = pl.pallas_call(
            functools.partial(_mm_residual_kernel, scale=scale),
            grid=(m // tm, n // tn),
            in_specs=in_specs,
            out_specs=pl.BlockSpec((tm, tn), lambda i, j: (i + blk0, j)),
            out_shape=jax.ShapeDtypeStruct((out_rows, n), F32),
            input_output_aliases=aliases,
            compiler_params=_params(2),
            name="mm_residual",
        )(*args)
    return dst


ROT_HALF = ROT_DIM // 2
PARTNER_LANE = HEAD_DIM // 2


def _pair_rotary_lanes(w):
    k, n = w.shape
    wh = w.reshape(k, n // HEAD_DIM, HEAD_DIM)
    a, b = ROT_HALF, PARTNER_LANE
    return jnp.concatenate([wh[..., :a], wh[..., b:b + a], wh[..., 2 * a:b], wh[..., a:2 * a], wh[..., b + a:]],
                           axis=-1).reshape(k, n)


def _proj_rope_kernel(x_ref, w_ref, cos_ref, sin_ref, o_ref):
    acc = jnp.dot(x_ref[...], w_ref[...], preferred_element_type=F32)
    cos, sin = cos_ref[...], sin_ref[...]
    for h in range(acc.shape[1] // HEAD_DIM):
        sl = slice(h * HEAD_DIM, (h + 1) * HEAD_DIM)
        seg = acc[:, sl]
        o_ref[:, sl] = (seg * cos + pltpu.roll(seg, PARTNER_LANE, axis=1) * sin).astype(o_ref.dtype)


def _proj_plain_kernel(x_ref, w_ref, o_ref):
    o_ref[...] = jnp.dot(x_ref[...], w_ref[...], preferred_element_type=F32).astype(o_ref.dtype)


def _proj_gate_kernel(x_ref, w_ref, o_ref):
    acc = jnp.dot(x_ref[...], w_ref[...], preferred_element_type=F32)
    o_ref[...] = _sigmoid(acc).astype(o_ref.dtype)


def _proj(xn, w, kind, tn, tables=(), tm=1024):
    m, k = xn.shape
    n = w.shape[1]
    body = {"rope": _proj_rope_kernel, "plain": _proj_plain_kernel, "gate": _proj_gate_kernel}[kind]
    tab = pl.BlockSpec((tm, HEAD_DIM), lambda i, j: (i, 0))
    return pl.pallas_call(
        body,
        grid=(m // tm, n // tn),
        in_specs=[pl.BlockSpec((tm, k), lambda i, j: (i, 0)),
                  pl.BlockSpec((k, tn), lambda i, j: (0, j))] + [tab] * len(tables),
        out_specs=pl.BlockSpec((tm, tn), lambda i, j: (i, j)),
        out_shape=jax.ShapeDtypeStruct((m, n), BF16),
        compiler_params=_params(2),
        name=f"proj_{kind}",
    )(xn, w, *tables)


def _seq_bounds(geo, g0):
    lo = jnp.int32(geo.starts[-1])
    hi = jnp.int32(geo.ends[-1])
    for s, e in zip(reversed(geo.starts[:-1]), reversed(geo.ends[:-1])):
        lo = jnp.where(g0 < e, s, lo)
        hi = jnp.where(g0 < e, e, hi)
    return lo, hi


def _band_mask(tq, nk, halo, half_window, lo, hi, u0, stacked=1):
    assert tq & (tq - 1) == 0
    ii = lax.broadcasted_iota(jnp.int32, (stacked * tq, nk), 0) & (tq - 1)
    jj = lax.broadcasted_iota(jnp.int32, (stacked * tq, nk), 1)
    rel = jj - ii - halo
    return ((rel >= -half_window) & (rel <= half_window)
            & (jj >= lo - u0 + halo) & (jj < hi - u0 + halo))


A_TQ = 128
A_STEP = 256


def _attn_a_kernel(q_ref, kp_ref, kc_ref, kn_ref, vp_ref, vc_ref, vn_ref, o_ref, l_ref, *, dil, geo):
    step0 = pl.program_id(0) * A_STEP
    plane_rows = geo.m // dil
    base = (step0 // plane_rows) * plane_rows
    lo, hi = _seq_bounds(geo, (step0 - base) * dil)
    lo, hi = base + lo // dil, base + hi // dil
    nk = A_TQ + 2 * A_HALF
    k = jnp.concatenate([kp_ref[...], kc_ref[...], kn_ref[...]], axis=0)
    v = jnp.concatenate([vp_ref[...], vc_ref[...], vn_ref[...]], axis=0)
    lane = lax.broadcasted_iota(jnp.int32, (A_TQ, HEAD_DIM), 1)
    for sub in range(A_STEP // A_TQ):
        rows = slice(sub * A_TQ, (sub + 1) * A_TQ)
        keys = slice(sub * A_TQ, sub * A_TQ + nk)
        valid = _band_mask(A_TQ, nk, A_HALF, A_HALF, lo, hi, step0 + sub * A_TQ)
        lse_tile = jnp.zeros((A_TQ, HEAD_DIM), F32)
        for h in range(A_HEADS):
            sl = slice(h * HEAD_DIM, (h + 1) * HEAD_DIM)
            s = lax.dot_general(q_ref[rows, sl], k[keys, sl], (((1,), (1,)), ((), ())),
                                preferred_element_type=F32) * SCALE
            s = jnp.where(valid, s, NEG_INF)
            m = jnp.max(s, axis=-1, keepdims=True)
            p = jnp.exp(s - m)
            den = jnp.sum(p, axis=-1, keepdims=True)
            o = jnp.dot(p.astype(BF16), v[keys, sl], preferred_element_type=F32)
            o_ref[rows, sl] = (o * (1.0 / den)).astype(o_ref.dtype)
            lse_tile = jnp.where(lane == h, m + jnp.log(den), lse_tile)
        l_ref[rows, :] = lse_tile


def _attn_a(geo, dil, q_arr, q_blk, k_arr, k_blk, v_arr, v_blk):
    m = geo.m
    hb = A_STEP // A_HALF
    last_halo = m // A_HALF - 1
    assert all(((e - s) // dil) % A_STEP == 0 for s, e in zip(geo.starts, geo.ends))

    def main(blk):
        return pl.BlockSpec((A_STEP, A_WIDTH), lambda t: (t, blk))

    def prev(blk):
        return pl.BlockSpec((A_HALF, A_WIDTH), lambda t: (jnp.maximum(t * hb - 1, 0), blk))

    def nxt(blk):
        return pl.BlockSpec((A_HALF, A_WIDTH), lambda t: (jnp.minimum(t * hb + hb, last_halo), blk))

    return pl.pallas_call(
        functools.partial(_attn_a_kernel, dil=dil, geo=geo),
        grid=(m // A_STEP,),
        in_specs=[main(q_blk), prev(k_blk), main(k_blk), nxt(k_blk),
                  prev(v_blk), main(v_blk), nxt(v_blk)],
        out_specs=[pl.BlockSpec((A_STEP, A_WIDTH), lambda t: (t, 0)),
                   pl.BlockSpec((A_STEP, HEAD_DIM), lambda t: (t, 0))],
        out_shape=[jax.ShapeDtypeStruct((m, A_WIDTH), BF16),
                   jax.ShapeDtypeStruct((m, HEAD_DIM), F32)],
        compiler_params=_params(1),
        name=f"attn_a_d{dil}",
    )(q_arr, k_arr, k_arr, k_arr, v_arr, v_arr, v_arr)


COMBINE_TM = 256


def _combine_a_kernel(o0_ref, l0_ref, o1_ref, l1_ref, o2_ref, l2_ref, y_ref, l1_s, l2_s):
    outs = [o0_ref[...].astype(F32)]
    for o_ref, l_ref, l_s, dil in ((o1_ref, l1_ref, l1_s, DIL_GROUPS[1][1]), (o2_ref, l2_ref, l2_s, DIL_GROUPS[2][1])):
        per = COMBINE_TM // dil
        for r in range(dil):
            l_s[pl.ds(r, per, stride=dil), :] = l_ref[r]
        planes = o_ref[...].reshape(COMBINE_TM, A_WIDTH)
        outs.append(jnp.dot(_plane_permutation(COMBINE_TM, per), planes, preferred_element_type=F32))
    a0, a1, a2 = l0_ref[...], l1_s[...], l2_s[...]
    mx = jnp.maximum(jnp.maximum(a0, a1), a2)
    e0, e1, e2 = jnp.exp(a0 - mx), jnp.exp(a1 - mx), jnp.exp(a2 - mx)
    inv = 1.0 / (e0 + e1 + e2)
    weights = (e0 * inv, e1 * inv, e2 * inv)
    for h in range(A_HEADS):
        sl = slice(h * HEAD_DIM, (h + 1) * HEAD_DIM)
        y = sum(jnp.broadcast_to(w[:, h:h + 1], (COMBINE_TM, HEAD_DIM)) * o[:, sl] for w, o in zip(weights, outs))
        y_ref[:, sl] = y.astype(y_ref.dtype)


def _combine_a(outs, lses):
    m = outs[0].shape[0]
    tm = COMBINE_TM
    d1, d2 = DIL_GROUPS[1][1], DIL_GROUPS[2][1]

    def nat(width):
        return pl.BlockSpec((tm, width), lambda i: (i, 0))

    def planes(dil, width):
        return pl.BlockSpec((dil, tm // dil, width), lambda i: (0, i, 0))

    lse_nat = pltpu.VMEM((tm, HEAD_DIM), F32)
    return pl.pallas_call(
        _combine_a_kernel,
        grid=(m // tm,),
        in_specs=[nat(A_WIDTH), nat(HEAD_DIM), planes(d1, A_WIDTH), planes(d1, HEAD_DIM),
                  planes(d2, A_WIDTH), planes(d2, HEAD_DIM)],
        out_specs=nat(A_WIDTH),
        out_shape=jax.ShapeDtypeStruct((m, A_WIDTH), BF16),
        scratch_shapes=[lse_nat, lse_nat],
        compiler_params=_params(1),
        name="combine_a",
    )(outs[0], lses[0],
      outs[1].reshape(d1, m // d1, A_WIDTH), lses[1].reshape(d1, m // d1, HEAD_DIM),
      outs[2].reshape(d2, m // d2, A_WIDTH), lses[2].reshape(d2, m // d2, HEAD_DIM))


B_TQ = 128


def _attn_b_kernel(sink_ref, q_ref, kp_ref, kc_ref, kn_ref, vp_ref, vc_ref, vn_ref, o_ref, *, geo):
    u0 = pl.program_id(0) * B_TQ
    lo, hi = _seq_bounds(geo, u0)
    nk = 3 * B_TQ
    valid = _band_mask(B_TQ, nk, B_TQ, B_WINDOW, lo, hi, u0, stacked=B_GROUP)
    k = jnp.concatenate([kp_ref[...], kc_ref[...], kn_ref[...]], axis=0)
    v = jnp.concatenate([vp_ref[...], vc_ref[...], vn_ref[...]], axis=0)
    row_head = lax.broadcasted_iota(jnp.int32, (B_GROUP * B_TQ, 1), 0) >> (B_TQ.bit_length() - 1)
    for g in range(B_KV_HEADS):
        heads = [slice((g * B_GROUP + h) * HEAD_DIM, (g * B_GROUP + h + 1) * HEAD_DIM) for h in range(B_GROUP)]
        gl = slice(g * HEAD_DIM, (g + 1) * HEAD_DIM)
        q = jnp.concatenate([q_ref[:, sl] for sl in heads], axis=0)
        sk = jnp.zeros((B_GROUP * B_TQ, 1), F32)
        for h in range(B_GROUP):
            sk = jnp.where(row_head == h, sink_ref[g * B_GROUP + h], sk)
        s = lax.dot_general(q, k[:, gl], (((1,), (1,)), ((), ())), preferred_element_type=F32) * SCALE
        s = jnp.where(valid, s, NEG_INF)
        m = jnp.maximum(jnp.max(s, axis=-1, keepdims=True), sk)
        p = jnp.exp(s - m)
        den = jnp.sum(p, axis=-1, keepdims=True) + jnp.exp(sk - m)
        o = jnp.dot(p.astype(BF16), v[:, gl], preferred_element_type=F32) * (1.0 / den)
        for h, sl in enumerate(heads):
            o_ref[:, sl] = o[h * B_TQ:(h + 1) * B_TQ].astype(o_ref.dtype)


def _attn_b(geo, sink, q_arr, q_blk, k_arr, k_blk, v_arr, v_blk):
    m = geo.m
    n_tiles = m // B_TQ

    def kv(blk, shift):
        return pl.BlockSpec((B_TQ, B_KV_WIDTH), lambda t: (jnp.clip(t + shift, 0, n_tiles - 1), blk))

    return pl.pallas_call(
        functools.partial(_attn_b_kernel, geo=geo),
        grid=(n_tiles,),
        in_specs=[pl.BlockSpec(memory_space=pltpu.SMEM),
                  pl.BlockSpec((B_TQ, B_Q_WIDTH), lambda t: (t, q_blk)),
                  kv(k_blk, -1), kv(k_blk, 0), kv(k_blk, 1),
                  kv(v_blk, -1), kv(v_blk, 0), kv(v_blk, 1)],
        out_specs=pl.BlockSpec((B_TQ, B_Q_WIDTH), lambda t: (t, 0)),
        out_shape=jax.ShapeDtypeStruct((m, B_Q_WIDTH), BF16),
        compiler_params=_params(1),
        name="attn_b",
    )(sink, q_arr, k_arr, k_arr, k_arr, v_arr, v_arr, v_arr)


def _merge_kernel(ya_ref, yb_ref, wa_ref, wb_ref, ga_ref, gb_ref, o_ref):
    a = jnp.dot(ya_ref[...], wa_ref[...], preferred_element_type=F32)
    b = jnp.dot(yb_ref[...], wb_ref[...], preferred_element_type=F32)
    o_ref[...] = (ga_ref[...].astype(F32) * a + gb_ref[...].astype(F32) * b).astype(o_ref.dtype)


def _merge(ya, yb, wa, wb, gates, tm=1024, tn=1024):
    m = ya.shape[0]
    n = wa.shape[1]
    return pl.pallas_call(
        _merge_kernel,
        grid=(m // tm, n // tn),
        in_specs=[pl.BlockSpec((tm, ya.shape[1]), lambda i, j: (i, 0)),
                  pl.BlockSpec((tm, yb.shape[1]), lambda i, j: (i, 0)),
                  pl.BlockSpec((wa.shape[0], tn), lambda i, j: (0, j)),
                  pl.BlockSpec((wb.shape[0], tn), lambda i, j: (0, j)),
                  pl.BlockSpec((tm, tn), lambda i, j: (i, j)),
                  pl.BlockSpec((tm, tn), lambda i, j: (i, n // tn + j))],
        out_specs=pl.BlockSpec((tm, tn), lambda i, j: (i, j)),
        out_shape=jax.ShapeDtypeStruct((m, n), BF16),
        compiler_params=_params(2),
        name="merge",
    )(ya, yb, wa, wb, gates, gates)


def _rope_tables(geo, dil):
    half = ROT_HALF
    pos = jnp.concatenate([jnp.arange(e - s) for s, e in zip(geo.starts, geo.ends)])
    pos = pos.reshape(geo.m // dil, dil).T.reshape(-1)
    inv = ROPE_THETA ** (-jnp.arange(half, dtype=F32) / half)
    ang = pos.astype(F32)[:, None] * inv[None, :]
    cos, sin = jnp.cos(ang), jnp.sin(ang)
    gap = jnp.zeros((geo.m, PARTNER_LANE - half), F32)
    cos_t = jnp.concatenate([cos, gap + 1.0, cos, gap + 1.0], axis=1)
    sin_t = jnp.concatenate([-sin, gap, sin, gap], axis=1)
    return cos_t, sin_t


def _mixers(geo, h, g_mix, w_in, sink):
    n_nat, n_d4, n_d16 = _rmsnorm_planes(h, g_mix)
    sources = (n_nat, n_d4, n_d16)

    qa, ka, va = (w_in[:, i * 3 * A_WIDTH:(i + 1) * 3 * A_WIDTH] for i in range(3))
    off = 9 * A_WIDTH
    qb = w_in[:, off:off + B_Q_WIDTH]
    kb = w_in[:, off + B_Q_WIDTH:off + B_Q_WIDTH + B_KV_WIDTH]
    vb = w_in[:, off + B_Q_WIDTH + B_KV_WIDTH:off + B_Q_WIDTH + 2 * B_KV_WIDTH]
    wg = w_in[:, off + B_Q_WIDTH + 2 * B_KV_WIDTH:]

    def grp(w, gi):
        return w[:, gi * A_WIDTH:(gi + 1) * A_WIDTH]

    def cat(*ws):
        return jnp.concatenate(ws, axis=1).astype(BF16)

    def cat_qk(*ws):
        return _pair_rotary_lanes(jnp.concatenate(ws, axis=1)).astype(BF16)

    rope0 = _proj(n_nat, cat_qk(grp(qa, 0), grp(ka, 0), qb, kb), "rope", B_KV_WIDTH, _rope_tables(geo, 1))
    plain0 = _proj(n_nat, cat(grp(va, 0), vb), "plain", B_KV_WIDTH)
    gates = _proj(n_nat, wg.astype(BF16), "gate", 1024)

    outs, lses = [], []
    o, l = _attn_a(geo, 1, rope0, 0, rope0, 1, plain0, 0)
    outs.append(o)
    lses.append(l)
    for gi in (1, 2):
        dil = DIL_GROUPS[gi][1]
        rope = _proj(sources[gi], cat_qk(grp(qa, gi), grp(ka, gi)), "rope", 1024, _rope_tables(geo, dil))
        plain = _proj(sources[gi], grp(va, gi).astype(BF16), "plain", 1024)
        o, l = _attn_a(geo, dil, rope, 0, rope, 1, plain, 0)
        outs.append(o)
        lses.append(l)
    ya = _combine_a(outs, lses)

    yb = _attn_b(geo, sink, rope0, 2 * A_WIDTH // B_Q_WIDTH, rope0, (2 * A_WIDTH + B_Q_WIDTH) // B_KV_WIDTH,
                 plain0, A_WIDTH // B_KV_WIDTH)
    return ya, yb, gates


def _ffn_half_step(parts, total_rows, g, w_gate, w_up, w_down):
    wg, wu, wd = w_gate.astype(BF16), w_up.astype(BF16), w_down.astype(BF16)
    out, row0 = None, 0
    for x in parts:
        act = _ffn_up(_rmsnorm(x, g, BF16), wg, wu)
        out = _mm_residual(act, wd, x, 0.5, tm=1024, tn=512, k_splits=2,
                           out_rows=total_rows, out_row0=row0, dst=out)
        row0 += x.shape[0]
    return out


def _layer(geo, parts, g_ffn1, w1_gate, w1_up, w1_down, g_mix, w_in, sink_b, w_branch_a, w_branch_b,
           w_out, g_ffn2, w2_gate, w2_up, w2_down, g_final):
    h = _ffn_half_step(parts, geo.m, g_ffn1, w1_gate, w1_up, w1_down)
    ya, yb, gates = _mixers(geo, h, g_mix, w_in, sink_b)
    merged = _merge(ya, yb, w_branch_a.astype(BF16), w_branch_b.astype(BF16), gates)
    h2 = _mm_residual(merged, w_out.astype(BF16), h, 1.0, tm=1024, tn=1024)
    x2 = _ffn_half_step([h2], geo.m, g_ffn2, w2_gate, w2_up, w2_down)
    outs, row0 = [], 0
    for x in parts:
        outs.append(_rmsnorm(x2, g_final, F32, row0=row0, rows=x.shape[0]))
        row0 += x.shape[0]
    return outs


def kernel(x_prompt, x_sample, g_ffn1, w1_gate, w1_up, w1_down, g_mix, w_in, sink_b, w_branch_a, w_branch_b, w_out, g_ffn2, w2_gate, w2_up, w2_down, g_final):
    assert w1_gate.shape[0] == 1, "single layer"
    bp, sp, _ = x_prompt.shape
    bs, ss, _ = x_sample.shape
    lens = (sp,) * bp + (ss,) * bs
    ends = tuple(sum(lens[:i + 1]) for i in range(len(lens)))
    geo = Geometry(starts=(0,) + ends[:-1], ends=ends)
    parts = [x_prompt.reshape(-1, D_MODEL), x_sample.reshape(-1, D_MODEL)]
    y_prompt, y_sample = _layer(
        geo, parts, g_ffn1[0], w1_gate[0], w1_up[0], w1_down[0], g_mix[0], w_in[0], sink_b[0],
        w_branch_a[0], w_branch_b[0], w_out[0], g_ffn2[0], w2_gate[0], w2_up[0], w2_down[0], g_final)
    return (y_prompt.reshape(x_prompt.shape), y_sample.reshape(x_sample.shape))
```

```python
import functools
import math
from typing import NamedTuple

import jax
import jax.numpy as jnp
from jax import lax
from jax.experimental import pallas as pl
from jax.experimental.pallas import tpu as pltpu

F32 = jnp.float32
BF16 = jnp.bfloat16

D_MODEL = 4096
D_FF = 11008
HEAD_DIM = 128
ROT_DIM = HEAD_DIM // 4
ROPE_THETA = 500000.0
DIL_GROUPS = ((128, 1), (512, 4), (2048, 16))
A_HEADS = 8
A_WIDTH = A_HEADS * HEAD_DIM
A_HALF = 64
B_Q_HEADS = 16
B_KV_HEADS = 4
B_GROUP = B_Q_HEADS // B_KV_HEADS
B_Q_WIDTH = B_Q_HEADS * HEAD_DIM
B_KV_WIDTH = B_KV_HEADS * HEAD_DIM
B_WINDOW = 128
EPS = 1e-6
NEG_INF = -1e30
SCALE = 1.0 / math.sqrt(HEAD_DIM)
assert all((win // 2) // dil == A_HALF for win, dil in DIL_GROUPS)

VMEM_LIMIT = 58 * 1024 * 1024


class Geometry(NamedTuple):
    starts: tuple
    ends: tuple

    @property
    def m(self):
        return self.ends[-1]


def _params(n_grid):
    return pltpu.CompilerParams(
        dimension_semantics=("arbitrary",) * n_grid,
        vmem_limit_bytes=VMEM_LIMIT)


def _normalise(x, g):
    var = jnp.mean(x * x, axis=-1, keepdims=True)
    return (x * lax.rsqrt(var + EPS)) * g


def _rmsnorm_kernel(x_ref, g_ref, o_ref):
    o_ref[...] = _normalise(x_ref[...], g_ref[...]).astype(o_ref.dtype)


def _rmsnorm(x, g, out_dtype, row0=0, rows=None, tm=256):
    d = x.shape[1]
    rows = x.shape[0] if rows is None else rows
    blk0 = row0 // tm
    assert row0 % tm == 0 and rows % tm == 0
    return pl.pallas_call(
        _rmsnorm_kernel,
        grid=(rows // tm,),
        in_specs=[pl.BlockSpec((tm, d), lambda i: (i + blk0, 0)),
                  pl.BlockSpec((1, d), lambda i: (0, 0))],
        out_specs=pl.BlockSpec((tm, d), lambda i: (i, 0)),
        out_shape=jax.ShapeDtypeStruct((rows, d), out_dtype),
        compiler_params=_params(1),
        name="rmsnorm",
    )(x, g.reshape(1, d))


NORM_TM = 256


def _plane_permutation(tm, dil):
    per = tm // dil
    assert per & (per - 1) == 0, "rows per plane must be a power of two"
    i = lax.broadcasted_iota(jnp.int32, (tm, tm), 0)
    j = lax.broadcasted_iota(jnp.int32, (tm, tm), 1)
    src = (i & (per - 1)) * dil + (i >> (per.bit_length() - 1))
    return (j == src).astype(BF16)


def _rmsnorm_planes_kernel(x_ref, g_ref, n_ref, n4_ref, n16_ref):
    y = _normalise(x_ref[...], g_ref[...]).astype(BF16)
    n_ref[...] = y
    for ref, dil in ((n4_ref, DIL_GROUPS[1][1]), (n16_ref, DIL_GROUPS[2][1])):
        perm = jnp.dot(_plane_permutation(NORM_TM, dil), y, preferred_element_type=F32).astype(BF16)
        ref[...] = perm.reshape(dil, NORM_TM // dil, y.shape[1])


def _rmsnorm_planes(x, g):
    m, d = x.shape
    tm = NORM_TM
    d4, d16 = DIL_GROUPS[1][1], DIL_GROUPS[2][1]
    n, n4, n16 = pl.pallas_call(
        _rmsnorm_planes_kernel,
        grid=(m // tm,),
        in_specs=[pl.BlockSpec((tm, d), lambda i: (i, 0)),
                  pl.BlockSpec((1, d), lambda i: (0, 0))],
        out_specs=[pl.BlockSpec((tm, d), lambda i: (i, 0)),
                   pl.BlockSpec((d4, tm // d4, d), lambda i: (0, i, 0)),
                   pl.BlockSpec((d16, tm // d16, d), lambda i: (0, i, 0))],
        out_shape=[jax.ShapeDtypeStruct((m, d), BF16),
                   jax.ShapeDtypeStruct((d4, m // d4, d), BF16),
                   jax.ShapeDtypeStruct((d16, m // d16, d), BF16)],
        compiler_params=_params(1),
        name="rmsnorm_planes",
    )(x, g.reshape(1, d))
    return n, n4.reshape(m, d), n16.reshape(m, d)


def _sigmoid(x):
    return 0.5 * jnp.tanh(0.5 * x) + 0.5


def _ffn_up_kernel(x_ref, wg_ref, wu_ref, *rest):
    o_ref = rest[-1]
    x = x_ref[...]
    g = jnp.dot(x, wg_ref[...], preferred_element_type=F32)
    u = jnp.dot(x, wu_ref[...], preferred_element_type=F32)
    o_ref[...] = (g * _sigmoid(g) * u).astype(o_ref.dtype)


FFN_TM = 1024
FFN_TN = 512
FFN_TAIL = D_FF % FFN_TN
assert FFN_TAIL % HEAD_DIM == 0 and (D_FF - FFN_TAIL) % FFN_TAIL == 0


def _ffn_up(xn, wg, wu):
    m, k = xn.shape
    n = wg.shape[1]
    tm = FFN_TM
    x_spec = pl.BlockSpec((tm, k), lambda i, j: (i, 0))
    act = pl.pallas_call(
        _ffn_up_kernel,
        grid=(m // tm, n // FFN_TN),
        in_specs=[x_spec,
                  pl.BlockSpec((k, FFN_TN), lambda i, j: (0, j)),
                  pl.BlockSpec((k, FFN_TN), lambda i, j: (0, j))],
        out_specs=pl.BlockSpec((tm, FFN_TN), lambda i, j: (i, j)),
        out_shape=jax.ShapeDtypeStruct((m, n), BF16),
        compiler_params=_params(2),
        name="ffn_up",
    )(xn, wg, wu)
    tail_block = (n - FFN_TAIL) // FFN_TAIL
    w_tail = pl.BlockSpec((k, FFN_TAIL), lambda i, j: (0, tail_block))
    return pl.pallas_call(
        _ffn_up_kernel,
        grid=(m // tm, 1),
        in_specs=[x_spec, w_tail, w_tail, pl.BlockSpec(memory_space=pl.ANY)],
        out_specs=pl.BlockSpec((tm, FFN_TAIL), lambda i, j: (i, tail_block)),
        out_shape=jax.ShapeDtypeStruct((m, n), BF16),
        input_output_aliases={3: 0},
        compiler_params=_params(2),
        name="ffn_up_tail",
    )(xn, wg, wu, act)


def _mm_residual_kernel(x_ref, w_ref, r_ref, *rest, scale):
    o_ref = rest[-1]
    acc = jnp.dot(x_ref[...], w_ref[...], preferred_element_type=F32)
    o_ref[...] = r_ref[...] + scale * acc


def _mm_residual(x, w, res, scale, tm, tn, k_splits=1, out_rows=None, out_row0=0, dst=None):
    m, k = x.shape
    n = w.shape[1]
    out_rows = m if out_rows is None else out_rows
    blk0 = out_row0 // tm
    tk = k // k_splits
    assert out_row0 % tm == 0 and k % k_splits == 0 and tk % HEAD_DIM == 0
    for s in range(k_splits):
        first = s == 0
        args = [x, w, res if first else dst]
        in_specs = [pl.BlockSpec((tm, tk), lambda i, j, s=s: (i, s)),
                    pl.BlockSpec((tk, tn), lambda i, j, s=s: (s, j)),
                    pl.BlockSpec((tm, tn), (lambda i, j: (i, j)) if first else (lambda i, j: (i + blk0, j)))]
        aliases = {}
        if first and dst is not None:
            args.append(dst)
            in_specs.append(pl.BlockSpec(memory_space=pl.ANY))
            aliases = {3: 0}
        elif not first:
            aliases = {2: 0}
        dst = pl.pallas_call(
            functools.partial(_mm_residual_kernel, scale=scale),
            grid=(m // tm, n // tn),
            in_specs=in_specs,
            out_specs=pl.BlockSpec((tm, tn), lambda i, j: (i + blk0, j)),
            out_shape=jax.ShapeDtypeStruct((out_rows, n), F32),
            input_output_aliases=aliases,
            compiler_params=_params(2),
            name="mm_residual",
        )(*args)
    return dst


ROT_HALF = ROT_DIM // 2
PARTNER_LANE = HEAD_DIM // 2


PAIR_TN = 512


def _pair_rotary_lanes_kernel(tbl_ref, w_ref, o_ref):
    src = lax.broadcasted_iota(jnp.int32, (PAIR_TN, PAIR_TN), 0)
    dst = lax.broadcasted_iota(jnp.int32, (PAIR_TN, PAIR_TN), 1)
    lane = dst & (HEAD_DIM - 1)
    shift = PARTNER_LANE - ROT_HALF
    want = dst + jnp.where((lane >= ROT_HALF) & (lane < ROT_DIM), shift, 0) \
               - jnp.where((lane >= PARTNER_LANE) & (lane < PARTNER_LANE + ROT_HALF), shift, 0)
    perm = (src == want).astype(BF16)
    o_ref[...] = jnp.dot(w_ref[...], perm, preferred_element_type=F32).astype(o_ref.dtype)


def _pair_rotary_lanes(w, col_blocks):
    k = w.shape[0]
    grid_spec = pltpu.PrefetchScalarGridSpec(
        num_scalar_prefetch=1,
        grid=(len(col_blocks),),
        in_specs=[pl.BlockSpec((k, PAIR_TN), lambda j, tbl: (0, tbl[j]))],
        out_specs=pl.BlockSpec((k, PAIR_TN), lambda j, tbl: (0, j)))
    return pl.pallas_call(
        _pair_rotary_lanes_kernel,
        grid_spec=grid_spec,
        out_shape=jax.ShapeDtypeStruct((k, len(col_blocks) * PAIR_TN), w.dtype),
        compiler_params=_params(1),
        name="pair_rotary_lanes",
    )(jnp.asarray(col_blocks, jnp.int32), w)


def _proj_rope_kernel(tbl_ref, x_ref, w_ref, cos_ref, sin_ref, o_ref):
    acc = jnp.dot(x_ref[...], w_ref[...], preferred_element_type=F32)
    cos, sin = cos_ref[...], sin_ref[...]
    for h in range(acc.shape[1] // HEAD_DIM):
        sl = slice(h * HEAD_DIM, (h + 1) * HEAD_DIM)
        seg = acc[:, sl]
        o_ref[:, sl] = (seg * cos + pltpu.roll(seg, PARTNER_LANE, axis=1) * sin).astype(o_ref.dtype)


def _proj_plain_kernel(tbl_ref, x_ref, w_ref, o_ref):
    o_ref[...] = jnp.dot(x_ref[...], w_ref[...], preferred_element_type=F32).astype(o_ref.dtype)


def _proj_gate_kernel(tbl_ref, x_ref, w_ref, o_ref):
    acc = jnp.dot(x_ref[...], w_ref[...], preferred_element_type=F32)
    o_ref[...] = _sigmoid(acc).astype(o_ref.dtype)


def _proj(xn, w, col_blocks, kind, tn, tables=(), tm=1024):
    m, k = xn.shape
    body = {"rope": _proj_rope_kernel, "plain": _proj_plain_kernel, "gate": _proj_gate_kernel}[kind]
    tab = pl.BlockSpec((tm, HEAD_DIM), lambda i, j, tbl: (i, 0))
    grid_spec = pltpu.PrefetchScalarGridSpec(
        num_scalar_prefetch=1,
        grid=(m // tm, len(col_blocks)),
        in_specs=[pl.BlockSpec((tm, k), lambda i, j, tbl: (i, 0)),
                  pl.BlockSpec((k, tn), lambda i, j, tbl: (0, tbl[j]))] + [tab] * len(tables),
        out_specs=pl.BlockSpec((tm, tn), lambda i, j, tbl: (i, j)))
    return pl.pallas_call(
        body,
        grid_spec=grid_spec,
        out_shape=jax.ShapeDtypeStruct((m, len(col_blocks) * tn), BF16),
        compiler_params=_params(2),
        name=f"proj_{kind}",
    )(jnp.asarray(col_blocks, jnp.int32), xn, w, *tables)


def _seq_bounds(geo, g0):
    lo = jnp.int32(geo.starts[-1])
    hi = jnp.int32(geo.ends[-1])
    for s, e in zip(reversed(geo.starts[:-1]), reversed(geo.ends[:-1])):
        lo = jnp.where(g0 < e, s, lo)
        hi = jnp.where(g0 < e, e, hi)
    return lo, hi


def _band_mask(tq, nk, halo, half_window, lo, hi, u0, stacked=1):
    assert tq & (tq - 1) == 0
    ii = lax.broadcasted_iota(jnp.int32, (stacked * tq, nk), 0) & (tq - 1)
    jj = lax.broadcasted_iota(jnp.int32, (stacked * tq, nk), 1)
    rel = jj - ii - halo
    return ((rel >= -half_window) & (rel <= half_window)
            & (jj >= lo - u0 + halo) & (jj < hi - u0 + halo))


A_TQ = 128
A_STEP = 256


def _attn_a_kernel(q_ref, kp_ref, kc_ref, kn_ref, vp_ref, vc_ref, vn_ref, o_ref, l_ref, *, dil, geo):
    step0 = pl.program_id(0) * A_STEP
    plane_rows = geo.m // dil
    base = (step0 // plane_rows) * plane_rows
    lo, hi = _seq_bounds(geo, (step0 - base) * dil)
    lo, hi = base + lo // dil, base + hi // dil
    nk = A_TQ + 2 * A_HALF
    k = jnp.concatenate([kp_ref[...], kc_ref[...], kn_ref[...]], axis=0)
    v = jnp.concatenate([vp_ref[...], vc_ref[...], vn_ref[...]], axis=0)
    lane = lax.broadcasted_iota(jnp.int32, (A_TQ, HEAD_DIM), 1)
    for sub in range(A_STEP // A_TQ):
        rows = slice(sub * A_TQ, (sub + 1) * A_TQ)
        keys = slice(sub * A_TQ, sub * A_TQ + nk)
        valid = _band_mask(A_TQ, nk, A_HALF, A_HALF, lo, hi, step0 + sub * A_TQ)
        lse_tile = jnp.zeros((A_TQ, HEAD_DIM), F32)
        for h in range(A_HEADS):
            sl = slice(h * HEAD_DIM, (h + 1) * HEAD_DIM)
            s = lax.dot_general(q_ref[rows, sl], k[keys, sl], (((1,), (1,)), ((), ())),
                                preferred_element_type=F32) * SCALE
            s = jnp.where(valid, s, NEG_INF)
            m = jnp.max(s, axis=-1, keepdims=True)
            p = jnp.exp(s - m)
            den = jnp.sum(p, axis=-1, keepdims=True)
            o = jnp.dot(p.astype(BF16), v[keys, sl], preferred_element_type=F32)
            o_ref[rows, sl] = (o * (1.0 / den)).astype(o_ref.dtype)
            lse_tile = jnp.where(lane == h, m + jnp.log(den), lse_tile)
        l_ref[rows, :] = lse_tile


def _attn_a(geo, dil, q_arr, q_blk, k_arr, k_blk, v_arr, v_blk):
    m = geo.m
    hb = A_STEP // A_HALF
    last_halo = m // A_HALF - 1
    assert all(((e - s) // dil) % A_STEP == 0 for s, e in zip(geo.starts, geo.ends))

    def main(blk):
        return pl.BlockSpec((A_STEP, A_WIDTH), lambda t: (t, blk))

    def prev(blk):
        return pl.BlockSpec((A_HALF, A_WIDTH), lambda t: (jnp.maximum(t * hb - 1, 0), blk))

    def nxt(blk):
        return pl.BlockSpec((A_HALF, A_WIDTH), lambda t: (jnp.minimum(t * hb + hb, last_halo), blk))

    return pl.pallas_call(
        functools.partial(_attn_a_kernel, dil=dil, geo=geo),
        grid=(m // A_STEP,),
        in_specs=[main(q_blk), prev(k_blk), main(k_blk), nxt(k_blk),
                  prev(v_blk), main(v_blk), nxt(v_blk)],
        out_specs=[pl.BlockSpec((A_STEP, A_WIDTH), lambda t: (t, 0)),
                   pl.BlockSpec((A_STEP, HEAD_DIM), lambda t: (t, 0))],
        out_shape=[jax.ShapeDtypeStruct((m, A_WIDTH), BF16),
                   jax.ShapeDtypeStruct((m, HEAD_DIM), F32)],
        compiler_params=_params(1),
        name=f"attn_a_d{dil}",
    )(q_arr, k_arr, k_arr, k_arr, v_arr, v_arr, v_arr)


COMBINE_TM = 256


def _combine_a_kernel(o0_ref, l0_ref, o1_ref, l1_ref, o2_ref, l2_ref, y_ref, l1_s, l2_s):
    outs = [o0_ref[...].astype(F32)]
    for o_ref, l_ref, l_s, dil in ((o1_ref, l1_ref, l1_s, DIL_GROUPS[1][1]), (o2_ref, l2_ref, l2_s, DIL_GROUPS[2][1])):
        per = COMBINE_TM // dil
        for r in range(dil):
            l_s[pl.ds(r, per, stride=dil), :] = l_ref[r]
        planes = o_ref[...].reshape(COMBINE_TM, A_WIDTH)
        outs.append(jnp.dot(_plane_permutation(COMBINE_TM, per), planes, preferred_element_type=F32))
    a0, a1, a2 = l0_ref[...], l1_s[...], l2_s[...]
    mx = jnp.maximum(jnp.maximum(a0, a1), a2)
    e0, e1, e2 = jnp.exp(a0 - mx), jnp.exp(a1 - mx), jnp.exp(a2 - mx)
    inv = 1.0 / (e0 + e1 + e2)
    weights = (e0 * inv, e1 * inv, e2 * inv)
    for h in range(A_HEADS):
        sl = slice(h * HEAD_DIM, (h + 1) * HEAD_DIM)
        y = sum(jnp.broadcast_to(w[:, h:h + 1], (COMBINE_TM, HEAD_DIM)) * o[:, sl] for w, o in zip(weights, outs))
        y_ref[:, sl] = y.astype(y_ref.dtype)


def _combine_a(outs, lses):
    m = outs[0].shape[0]
    tm = COMBINE_TM
    d1, d2 = DIL_GROUPS[1][1], DIL_GROUPS[2][1]

    def nat(width):
        return pl.BlockSpec((tm, width), lambda i: (i, 0))

    def planes(dil, width):
        return pl.BlockSpec((dil, tm // dil, width), lambda i: (0, i, 0))

    lse_nat = pltpu.VMEM((tm, HEAD_DIM), F32)
    return pl.pallas_call(
        _combine_a_kernel,
        grid=(m // tm,),
        in_specs=[nat(A_WIDTH), nat(HEAD_DIM), planes(d1, A_WIDTH), planes(d1, HEAD_DIM),
                  planes(d2, A_WIDTH), planes(d2, HEAD_DIM)],
        out_specs=nat(A_WIDTH),
        out_shape=jax.ShapeDtypeStruct((m, A_WIDTH), BF16),
        scratch_shapes=[lse_nat, lse_nat],
        compiler_params=_params(1),
        name="combine_a",
    )(outs[0], lses[0],
      outs[1].reshape(d1, m // d1, A_WIDTH), lses[1].reshape(d1, m // d1, HEAD_DIM),
      outs[2].reshape(d2, m // d2, A_WIDTH), lses[2].reshape(d2, m // d2, HEAD_DIM))


B_TQ = 128


def _attn_b_kernel(sink_ref, q_ref, kp_ref, kc_ref, kn_ref, vp_ref, vc_ref, vn_ref, o_ref, *, geo):
    u0 = pl.program_id(0) * B_TQ
    lo, hi = _seq_bounds(geo, u0)
    nk = 3 * B_TQ
    valid = _band_mask(B_TQ, nk, B_TQ, B_WINDOW, lo, hi, u0, stacked=B_GROUP)
    k = jnp.concatenate([kp_ref[...], kc_ref[...], kn_ref[...]], axis=0)
    v = jnp.concatenate([vp_ref[...], vc_ref[...], vn_ref[...]], axis=0)
    row_head = lax.broadcasted_iota(jnp.int32, (B_GROUP * B_TQ, 1), 0) >> (B_TQ.bit_length() - 1)
    for g in range(B_KV_HEADS):
        heads = [slice((g * B_GROUP + h) * HEAD_DIM, (g * B_GROUP + h + 1) * HEAD_DIM) for h in range(B_GROUP)]
        gl = slice(g * HEAD_DIM, (g + 1) * HEAD_DIM)
        q = jnp.concatenate([q_ref[:, sl] for sl in heads], axis=0)
        sk = jnp.zeros((B_GROUP * B_TQ, 1), F32)
        for h in range(B_GROUP):
            sk = jnp.where(row_head == h, sink_ref[g * B_GROUP + h], sk)
        s = lax.dot_general(q, k[:, gl], (((1,), (1,)), ((), ())), preferred_element_type=F32) * SCALE
        s = jnp.where(valid, s, NEG_INF)
        m = jnp.maximum(jnp.max(s, axis=-1, keepdims=True), sk)
        p = jnp.exp(s - m)
        den = jnp.sum(p, axis=-1, keepdims=True) + jnp.exp(sk - m)
        o = jnp.dot(p.astype(BF16), v[:, gl], preferred_element_type=F32) * (1.0 / den)
        for h, sl in enumerate(heads):
            o_ref[:, sl] = o[h * B_TQ:(h + 1) * B_TQ].astype(o_ref.dtype)


def _attn_b(geo, sink, q_arr, q_blk, k_arr, k_blk, v_arr, v_blk):
    m = geo.m
    n_tiles = m // B_TQ

    def kv(blk, shift):
        return pl.BlockSpec((B_TQ, B_KV_WIDTH), lambda t: (jnp.clip(t + shift, 0, n_tiles - 1), blk))

    return pl.pallas_call(
        functools.partial(_attn_b_kernel, geo=geo),
        grid=(n_tiles,),
        in_specs=[pl.BlockSpec(memory_space=pltpu.SMEM),
                  pl.BlockSpec((B_TQ, B_Q_WIDTH), lambda t: (t, q_blk)),
                  kv(k_blk, -1), kv(k_blk, 0), kv(k_blk, 1),
                  kv(v_blk, -1), kv(v_blk, 0), kv(v_blk, 1)],
        out_specs=pl.BlockSpec((B_TQ, B_Q_WIDTH), lambda t: (t, 0)),
        out_shape=jax.ShapeDtypeStruct((m, B_Q_WIDTH), BF16),
        compiler_params=_params(1),
        name="attn_b",
    )(sink, q_arr, k_arr, k_arr, k_arr, v_arr, v_arr, v_arr)


def _merge_kernel(ya_ref, yb_ref, wa_ref, wb_ref, ga_ref, gb_ref, o_ref):
    a = jnp.dot(ya_ref[...], wa_ref[...], preferred_element_type=F32)
    b = jnp.dot(yb_ref[...], wb_ref[...], preferred_element_type=F32)
    o_ref[...] = (ga_ref[...].astype(F32) * a + gb_ref[...].astype(F32) * b).astype(o_ref.dtype)


def _merge(ya, yb, wa, wb, gates, tm=1024, tn=1024):
    m = ya.shape[0]
    n = wa.shape[1]
    return pl.pallas_call(
        _merge_kernel,
        grid=(m // tm, n // tn),
        in_specs=[pl.BlockSpec((tm, ya.shape[1]), lambda i, j: (i, 0)),
                  pl.BlockSpec((tm, yb.shape[1]), lambda i, j: (i, 0)),
                  pl.BlockSpec((wa.shape[0], tn), lambda i, j: (0, j)),
                  pl.BlockSpec((wb.shape[0], tn), lambda i, j: (0, j)),
                  pl.BlockSpec((tm, tn), lambda i, j: (i, j)),
                  pl.BlockSpec((tm, tn), lambda i, j: (i, n // tn + j))],
        out_specs=pl.BlockSpec((tm, tn), lambda i, j: (i, j)),
        out_shape=jax.ShapeDtypeStruct((m, n), BF16),
        compiler_params=_params(2),
        name="merge",
    )(ya, yb, wa, wb, gates, gates)


def _rope_tables(geo, dils):
    half = ROT_HALF
    pos = jnp.concatenate([jnp.arange(e - s) for s, e in zip(geo.starts, geo.ends)])
    inv = ROPE_THETA ** (-jnp.arange(half, dtype=F32) / half)
    ang = pos.astype(F32)[:, None] * inv[None, :]
    cos_sin = jnp.concatenate([jnp.cos(ang), jnp.sin(ang)], axis=1)
    gap = jnp.zeros((geo.m, PARTNER_LANE - half), F32)
    tables = []
    for dil in dils:
        cs = cos_sin.reshape(geo.m // dil, dil, 2 * half).transpose(1, 0, 2).reshape(geo.m, 2 * half)
        cos, sin = cs[:, :half], cs[:, half:]
        cos_t = jnp.concatenate([cos, gap + 1.0, cos, gap + 1.0], axis=1)
        sin_t = jnp.concatenate([-sin, gap, sin, gap], axis=1)
        tables.append((cos_t, sin_t))
    return tables


def _mixers(geo, h, g_mix, w_in, sink):
    n_nat, n_d4, n_d16 = _rmsnorm_planes(h, g_mix)
    sources = (n_nat, n_d4, n_d16)
    wb = w_in.astype(BF16)
    tables = _rope_tables(geo, [dil for _, dil in DIL_GROUPS])

    qa_off, ka_off, va_off, qb_off = 0, 3 * A_WIDTH, 6 * A_WIDTH, 9 * A_WIDTH
    kb_off = qb_off + B_Q_WIDTH
    vb_off = kb_off + B_KV_WIDTH
    gate_off = vb_off + B_KV_WIDTH

    def blocks(start, width, tn):
        assert start % tn == 0 and width % tn == 0
        return tuple(range(start // tn, (start + width) // tn))

    def grp(off, gi, tn):
        return blocks(off + gi * A_WIDTH, A_WIDTH, tn)

    pt = PAIR_TN
    w_rope = _pair_rotary_lanes(wb, grp(qa_off, 1, pt) + grp(ka_off, 1, pt) + grp(qa_off, 2, pt) + grp(ka_off, 2, pt)
                                + grp(qa_off, 0, pt) + grp(ka_off, 0, pt)
                                + blocks(qb_off, B_Q_WIDTH, pt) + blocks(kb_off, B_KV_WIDTH, pt))

    tn0 = B_KV_WIDTH
    rope0 = _proj(n_nat, w_rope, blocks(4 * A_WIDTH, 2 * A_WIDTH + B_Q_WIDTH + B_KV_WIDTH, tn0), "rope", tn0, tables[0])
    plain0 = _proj(n_nat, wb, grp(va_off, 0, tn0) + blocks(vb_off, B_KV_WIDTH, tn0), "plain", tn0)
    gates = _proj(n_nat, wb, blocks(gate_off, 2 * D_MODEL, 1024), "gate", 1024)

    outs, lses = [], []
    o, l = _attn_a(geo, 1, rope0, 0, rope0, 1, plain0, 0)
    outs.append(o)
    lses.append(l)
    for gi in (1, 2):
        dil = DIL_GROUPS[gi][1]
        rope = _proj(sources[gi], w_rope, blocks((gi - 1) * 2 * A_WIDTH, 2 * A_WIDTH, A_WIDTH), "rope", A_WIDTH, tables[gi])
        plain = _proj(sources[gi], wb, grp(va_off, gi, A_WIDTH), "plain", A_WIDTH)
        o, l = _attn_a(geo, dil, rope, 0, rope, 1, plain, 0)
        outs.append(o)
        lses.append(l)
    ya = _combine_a(outs, lses)

    yb = _attn_b(geo, sink, rope0, 2 * A_WIDTH // B_Q_WIDTH, rope0, (2 * A_WIDTH + B_Q_WIDTH) // B_KV_WIDTH,
                 plain0, A_WIDTH // B_KV_WIDTH)
    return ya, yb, gates


def _ffn_half_step(parts, total_rows, g, w_gate, w_up, w_down):
    wg, wu, wd = w_gate.astype(BF16), w_up.astype(BF16), w_down.astype(BF16)
    out, row0 = None, 0
    for x in parts:
        act = _ffn_up(_rmsnorm(x, g, BF16), wg, wu)
        out = _mm_residual(act, wd, x, 0.5, tm=1024, tn=512, k_splits=2,
                           out_rows=total_rows, out_row0=row0, dst=out)
        row0 += x.shape[0]
    return out


def _layer(geo, parts, g_ffn1, w1_gate, w1_up, w1_down, g_mix, w_in, sink_b, w_branch_a, w_branch_b,
           w_out, g_ffn2, w2_gate, w2_up, w2_down, g_final):
    h = _ffn_half_step(parts, geo.m, g_ffn1, w1_gate, w1_up, w1_down)
    ya, yb, gates = _mixers(geo, h, g_mix, w_in, sink_b)
    merged = _merge(ya, yb, w_branch_a.astype(BF16), w_branch_b.astype(BF16), gates)
    h2 = _mm_residual(merged, w_out.astype(BF16), h, 1.0, tm=1024, tn=1024)
    x2 = _ffn_half_step([h2], geo.m, g_ffn2, w2_gate, w2_up, w2_down)
    outs, row0 = [], 0
    for x in parts:
        outs.append(_rmsnorm(x2, g_final, F32, row0=row0, rows=x.shape[0]))
        row0 += x.shape[0]
    return outs


def kernel(x_prompt, x_sample, g_ffn1, w1_gate, w1_up, w1_down, g_mix, w_in, sink_b, w_branch_a, w_branch_b, w_out, g_ffn2, w2_gate, w2_up, w2_down, g_final):
    assert w1_gate.shape[0] == 1, "single layer"
    bp, sp, _ = x_prompt.shape
    bs, ss, _ = x_sample.shape
    lens = (sp,) * bp + (ss,) * bs
    ends = tuple(sum(lens[:i + 1]) for i in range(len(lens)))
    geo = Geometry(starts=(0,) + ends[:-1], ends=ends)
    parts = [x_prompt.reshape(-1, D_MODEL), x_sample.reshape(-1, D_MODEL)]
    y_prompt, y_sample = _layer(
        geo, parts, g_ffn1[0], w1_gate[0], w1_up[0], w1_down[0], g_mix[0], w_in[0], sink_b[0],
        w_branch_a[0], w_branch_b[0], w_out[0], g_ffn2[0], w2_gate[0], w2_up[0], w2_down[0], g_final)
    return (y_prompt.reshape(x_prompt.shape), y_sample.reshape(x_sample.shape))
```

```python
import functools
import math
from typing import NamedTuple

import jax
import jax.numpy as jnp
from jax import lax
from jax.experimental import pallas as pl
from jax.experimental.pallas import tpu as pltpu

F32 = jnp.float32
BF16 = jnp.bfloat16

D_MODEL = 4096
D_FF = 11008
HEAD_DIM = 128
ROT_DIM = HEAD_DIM // 4
ROPE_THETA = 500000.0
DIL_GROUPS = ((128, 1), (512, 4), (2048, 16))
A_HEADS = 8
A_WIDTH = A_HEADS * HEAD_DIM
A_HALF = 64
B_Q_HEADS = 16
B_KV_HEADS = 4
B_GROUP = B_Q_HEADS // B_KV_HEADS
B_Q_WIDTH = B_Q_HEADS * HEAD_DIM
B_KV_WIDTH = B_KV_HEADS * HEAD_DIM
B_WINDOW = 128
EPS = 1e-6
NEG_INF = -1e30
SCALE = 1.0 / math.sqrt(HEAD_DIM)
assert all((win // 2) // dil == A_HALF for win, dil in DIL_GROUPS)

VMEM_LIMIT = 58 * 1024 * 1024


class Geometry(NamedTuple):
    starts: tuple
    ends: tuple

    @property
    def m(self):
        return self.ends[-1]


def _params(n_grid):
    return pltpu.CompilerParams(
        dimension_semantics=("arbitrary",) * n_grid,
        vmem_limit_bytes=VMEM_LIMIT)


def _normalise(x, g):
    var = jnp.mean(x * x, axis=-1, keepdims=True)
    return (x * lax.rsqrt(var + EPS)) * g


def _rmsnorm_kernel(x_ref, g_ref, o_ref):
    o_ref[...] = _normalise(x_ref[...], g_ref[...]).astype(o_ref.dtype)


def _rmsnorm(x, g, out_dtype, row0=0, rows=None, tm=256):
    d = x.shape[1]
    rows = x.shape[0] if rows is None else rows
    blk0 = row0 // tm
    assert row0 % tm == 0 and rows % tm == 0
    return pl.pallas_call(
        _rmsnorm_kernel,
        grid=(rows // tm,),
        in_specs=[pl.BlockSpec((tm, d), lambda i: (i + blk0, 0)),
                  pl.BlockSpec((1, d), lambda i: (0, 0))],
        out_specs=pl.BlockSpec((tm, d), lambda i: (i, 0)),
        out_shape=jax.ShapeDtypeStruct((rows, d), out_dtype),
        compiler_params=_params(1),
        name="rmsnorm",
    )(x, g.reshape(1, d))


NORM_TM = 256


def _plane_permutation(tm, dil):
    per = tm // dil
    assert per & (per - 1) == 0, "rows per plane must be a power of two"
    i = lax.broadcasted_iota(jnp.int32, (tm, tm), 0)
    j = lax.broadcasted_iota(jnp.int32, (tm, tm), 1)
    src = (i & (per - 1)) * dil + (i >> (per.bit_length() - 1))
    return (j == src).astype(BF16)


def _rmsnorm_planes_kernel(x_ref, g_ref, n_ref, n4_ref, n16_ref):
    y = _normalise(x_ref[...], g_ref[...]).astype(BF16)
    n_ref[...] = y
    for ref, dil in ((n4_ref, DIL_GROUPS[1][1]), (n16_ref, DIL_GROUPS[2][1])):
        perm = jnp.dot(_plane_permutation(NORM_TM, dil), y, preferred_element_type=F32).astype(BF16)
        ref[...] = perm.reshape(dil, NORM_TM // dil, y.shape[1])


def _rmsnorm_planes(x, g):
    m, d = x.shape
    tm = NORM_TM
    d4, d16 = DIL_GROUPS[1][1], DIL_GROUPS[2][1]
    n, n4, n16 = pl.pallas_call(
        _rmsnorm_planes_kernel,
        grid=(m // tm,),
        in_specs=[pl.BlockSpec((tm, d), lambda i: (i, 0)),
                  pl.BlockSpec((1, d), lambda i: (0, 0))],
        out_specs=[pl.BlockSpec((tm, d), lambda i: (i, 0)),
                   pl.BlockSpec((d4, tm // d4, d), lambda i: (0, i, 0)),
                   pl.BlockSpec((d16, tm // d16, d), lambda i: (0, i, 0))],
        out_shape=[jax.ShapeDtypeStruct((m, d), BF16),
                   jax.ShapeDtypeStruct((d4, m // d4, d), BF16),
                   jax.ShapeDtypeStruct((d16, m // d16, d), BF16)],
        compiler_params=_params(1),
        name="rmsnorm_planes",
    )(x, g.reshape(1, d))
    return n, n4.reshape(m, d), n16.reshape(m, d)


def _sigmoid(x):
    return 0.5 * jnp.tanh(0.5 * x) + 0.5


def _hosted_cast_spec(w, grid):
    rows, cols = w.shape
    steps = math.prod(grid)
    rb = next(b for b in range(16, rows + 1, 16) if rows % b == 0 and rows // b <= steps)
    last = rows // rb - 1

    def index(*ids):
        flat = ids[0]
        for extent, i in zip(grid[1:], ids[1:len(grid)]):
            flat = flat * extent + i
        return (jnp.minimum(flat, last), 0)

    return pl.BlockSpec((rb, cols), index)


def _hosting_cast(body, n_lead):
    def kernel(*refs):
        refs[-1][...] = refs[n_lead][...].astype(refs[-1].dtype)
        body(*refs[:n_lead], *refs[n_lead + 1:-1])
    return kernel


def _ffn_up_kernel(x_ref, wg_ref, wu_ref, *rest):
    o_ref = rest[-1]
    x = x_ref[...]
    g = jnp.dot(x, wg_ref[...], preferred_element_type=F32)
    u = jnp.dot(x, wu_ref[...], preferred_element_type=F32)
    o_ref[...] = (g * _sigmoid(g) * u).astype(o_ref.dtype)


FFN_TM = 1024
FFN_TN = 512
FFN_TAIL = D_FF % FFN_TN
assert FFN_TAIL % HEAD_DIM == 0 and (D_FF - FFN_TAIL) % FFN_TAIL == 0


def _ffn_up(xn, wg, wu, cast=None):
    m, k = xn.shape
    n = wg.shape[1]
    tm = FFN_TM
    grid = (m // tm, n // FFN_TN)
    x_spec = pl.BlockSpec((tm, k), lambda i, j: (i, 0))
    body, args = _ffn_up_kernel, [xn, wg, wu]
    in_specs = [x_spec,
                pl.BlockSpec((k, FFN_TN), lambda i, j: (0, j)),
                pl.BlockSpec((k, FFN_TN), lambda i, j: (0, j))]
    out_specs = [pl.BlockSpec((tm, FFN_TN), lambda i, j: (i, j))]
    out_shape = [jax.ShapeDtypeStruct((m, n), BF16)]
    if cast is not None:
        body = _hosting_cast(body, len(args))
        args.append(cast)
        in_specs.append(_hosted_cast_spec(cast, grid))
        out_specs.append(in_specs[-1])
        out_shape.append(jax.ShapeDtypeStruct(cast.shape, BF16))
    act, *converted = pl.pallas_call(
        body,
        grid=grid,
        in_specs=in_specs,
        out_specs=out_specs,
        out_shape=out_shape,
        compiler_params=_params(2),
        name="ffn_up",
    )(*args)
    tail_block = (n - FFN_TAIL) // FFN_TAIL
    w_tail = pl.BlockSpec((k, FFN_TAIL), lambda i, j: (0, tail_block))
    act = pl.pallas_call(
        _ffn_up_kernel,
        grid=(m // tm, 1),
        in_specs=[x_spec, w_tail, w_tail, pl.BlockSpec(memory_space=pl.ANY)],
        out_specs=pl.BlockSpec((tm, FFN_TAIL), lambda i, j: (i, tail_block)),
        out_shape=jax.ShapeDtypeStruct((m, n), BF16),
        input_output_aliases={3: 0},
        compiler_params=_params(2),
        name="ffn_up_tail",
    )(xn, wg, wu, act)
    return act, (converted[0] if converted else None)


def _mm_residual_kernel(x_ref, w_ref, r_ref, *rest, scale):
    o_ref = rest[-1]
    acc = jnp.dot(x_ref[...], w_ref[...], preferred_element_type=F32)
    o_ref[...] = r_ref[...] + scale * acc


def _mm_residual(x, w, res, scale, tm, tn, k_splits=1, out_rows=None, out_row0=0, dst=None, cast=None):
    m, k = x.shape
    n = w.shape[1]
    out_rows = m if out_rows is None else out_rows
    blk0 = out_row0 // tm
    tk = k // k_splits
    grid = (m // tm, n // tn)
    assert out_row0 % tm == 0 and k % k_splits == 0 and tk % HEAD_DIM == 0
    converted = None
    for s in range(k_splits):
        first = s == 0
        body = functools.partial(_mm_residual_kernel, scale=scale)
        args = [x, w, res if first else dst]
        in_specs = [pl.BlockSpec((tm, tk), lambda i, j, s=s: (i, s)),
                    pl.BlockSpec((tk, tn), lambda i, j, s=s: (s, j)),
                    pl.BlockSpec((tm, tn), (lambda i, j: (i, j)) if first else (lambda i, j: (i + blk0, j)))]
        out_specs = [pl.BlockSpec((tm, tn), lambda i, j: (i + blk0, j))]
        out_shape = [jax.ShapeDtypeStruct((out_rows, n), F32)]
        aliases = {}
        if first and dst is not None:
            args.append(dst)
            in_specs.append(pl.BlockSpec(memory_space=pl.ANY))
            aliases = {3: 0}
        elif not first:
            aliases = {2: 0}
        if first and cast is not None:
            body = _hosting_cast(body, len(args))
            args.append(cast)
            in_specs.append(_hosted_cast_spec(cast, grid))
            out_specs.append(in_specs[-1])
            out_shape.append(jax.ShapeDtypeStruct(cast.shape, BF16))
        dst, *extra = pl.pallas_call(
            body,
            grid=grid,
            in_specs=in_specs,
            out_specs=out_specs,
            out_shape=out_shape,
            input_output_aliases=aliases,
            compiler_params=_params(2),
            name="mm_residual",
        )(*args)
        converted = extra[0] if extra else converted
    return dst, converted


ROT_HALF = ROT_DIM // 2
PARTNER_LANE = HEAD_DIM // 2


PAIR_TN = 512


def _pair_rotary_lanes_kernel(tbl_ref, w_ref, o_ref):
    src = lax.broadcasted_iota(jnp.int32, (PAIR_TN, PAIR_TN), 0)
    dst = lax.broadcasted_iota(jnp.int32, (PAIR_TN, PAIR_TN), 1)
    lane = dst & (HEAD_DIM - 1)
    shift = PARTNER_LANE - ROT_HALF
    want = dst + jnp.where((lane >= ROT_HALF) & (lane < ROT_DIM), shift, 0) \
               - jnp.where((lane >= PARTNER_LANE) & (lane < PARTNER_LANE + ROT_HALF), shift, 0)
    perm = (src == want).astype(BF16)
    o_ref[...] = jnp.dot(w_ref[...], perm, preferred_element_type=F32).astype(o_ref.dtype)


def _pair_rotary_lanes(w, col_blocks):
    k = w.shape[0]
    grid_spec = pltpu.PrefetchScalarGridSpec(
        num_scalar_prefetch=1,
        grid=(len(col_blocks),),
        in_specs=[pl.BlockSpec((k, PAIR_TN), lambda j, tbl: (0, tbl[j]))],
        out_specs=pl.BlockSpec((k, PAIR_TN), lambda j, tbl: (0, j)))
    return pl.pallas_call(
        _pair_rotary_lanes_kernel,
        grid_spec=grid_spec,
        out_shape=jax.ShapeDtypeStruct((k, len(col_blocks) * PAIR_TN), w.dtype),
        compiler_params=_params(1),
        name="pair_rotary_lanes",
    )(jnp.asarray(col_blocks, jnp.int32), w)


def _proj_rope_kernel(tbl_ref, x_ref, w_ref, cos_ref, sin_ref, o_ref):
    acc = jnp.dot(x_ref[...], w_ref[...], preferred_element_type=F32)
    cos, sin = cos_ref[...], sin_ref[...]
    for h in range(acc.shape[1] // HEAD_DIM):
        sl = slice(h * HEAD_DIM, (h + 1) * HEAD_DIM)
        seg = acc[:, sl]
        o_ref[:, sl] = (seg * cos + pltpu.roll(seg, PARTNER_LANE, axis=1) * sin).astype(o_ref.dtype)


def _proj_plain_kernel(tbl_ref, x_ref, w_ref, o_ref):
    o_ref[...] = jnp.dot(x_ref[...], w_ref[...], preferred_element_type=F32).astype(o_ref.dtype)


def _proj_gate_kernel(tbl_ref, x_ref, w_ref, o_ref):
    acc = jnp.dot(x_ref[...], w_ref[...], preferred_element_type=F32)
    o_ref[...] = _sigmoid(acc).astype(o_ref.dtype)


def _proj(xn, w, col_blocks, kind, tn, tables=(), tm=1024, cast=None):
    m, k = xn.shape
    body = {"rope": _proj_rope_kernel, "plain": _proj_plain_kernel, "gate": _proj_gate_kernel}[kind]
    grid = (m // tm, len(col_blocks))
    tab = pl.BlockSpec((tm, HEAD_DIM), lambda i, j, tbl: (i, 0))
    args = [jnp.asarray(col_blocks, jnp.int32), xn, w, *tables]
    in_specs = [pl.BlockSpec((tm, k), lambda i, j, tbl: (i, 0)),
                pl.BlockSpec((k, tn), lambda i, j, tbl: (0, tbl[j]))] + [tab] * len(tables)
    out_specs = [pl.BlockSpec((tm, tn), lambda i, j, tbl: (i, j))]
    out_shape = [jax.ShapeDtypeStruct((m, len(col_blocks) * tn), BF16)]
    if cast is not None:
        body = _hosting_cast(body, len(args))
        args.append(cast)
        in_specs.append(_hosted_cast_spec(cast, grid))
        out_specs.append(in_specs[-1])
        out_shape.append(jax.ShapeDtypeStruct(cast.shape, BF16))
    outs = pl.pallas_call(
        body,
        grid_spec=pltpu.PrefetchScalarGridSpec(num_scalar_prefetch=1, grid=grid, in_specs=in_specs,
                                               out_specs=out_specs),
        out_shape=out_shape,
        compiler_params=_params(2),
        name=f"proj_{kind}",
    )(*args)
    return outs[0] if cast is None else tuple(outs)


def _seq_bounds(geo, g0):
    lo = jnp.int32(geo.starts[-1])
    hi = jnp.int32(geo.ends[-1])
    for s, e in zip(reversed(geo.starts[:-1]), reversed(geo.ends[:-1])):
        lo = jnp.where(g0 < e, s, lo)
        hi = jnp.where(g0 < e, e, hi)
    return lo, hi


def _band_mask(tq, nk, halo, half_window, lo, hi, u0, stacked=1):
    assert tq & (tq - 1) == 0
    ii = lax.broadcasted_iota(jnp.int32, (stacked * tq, nk), 0) & (tq - 1)
    jj = lax.broadcasted_iota(jnp.int32, (stacked * tq, nk), 1)
    rel = jj - ii - halo
    return ((rel >= -half_window) & (rel <= half_window)
            & (jj >= lo - u0 + halo) & (jj < hi - u0 + halo))


A_TQ = 128
A_STEP = 256


def _attn_a_kernel(q_ref, kp_ref, kc_ref, kn_ref, vp_ref, vc_ref, vn_ref, o_ref, l_ref, *, dil, geo):
    step0 = pl.program_id(0) * A_STEP
    plane_rows = geo.m // dil
    base = (step0 // plane_rows) * plane_rows
    lo, hi = _seq_bounds(geo, (step0 - base) * dil)
    lo, hi = base + lo // dil, base + hi // dil
    nk = A_TQ + 2 * A_HALF
    k = jnp.concatenate([kp_ref[...], kc_ref[...], kn_ref[...]], axis=0)
    v = jnp.concatenate([vp_ref[...], vc_ref[...], vn_ref[...]], axis=0)
    lane = lax.broadcasted_iota(jnp.int32, (A_TQ, HEAD_DIM), 1)
    for sub in range(A_STEP // A_TQ):
        rows = slice(sub * A_TQ, (sub + 1) * A_TQ)
        keys = slice(sub * A_TQ, sub * A_TQ + nk)
        valid = _band_mask(A_TQ, nk, A_HALF, A_HALF, lo, hi, step0 + sub * A_TQ)
        lse_tile = jnp.zeros((A_TQ, HEAD_DIM), F32)
        for h in range(A_HEADS):
            sl = slice(h * HEAD_DIM, (h + 1) * HEAD_DIM)
            s = lax.dot_general(q_ref[rows, sl], k[keys, sl], (((1,), (1,)), ((), ())),
                                preferred_element_type=F32) * SCALE
            s = jnp.where(valid, s, NEG_INF)
            m = jnp.max(s, axis=-1, keepdims=True)
            p = jnp.exp(s - m)
            den = jnp.sum(p, axis=-1, keepdims=True)
            o = jnp.dot(p.astype(BF16), v[keys, sl], preferred_element_type=F32)
            o_ref[rows, sl] = (o * (1.0 / den)).astype(o_ref.dtype)
            lse_tile = jnp.where(lane == h, m + jnp.log(den), lse_tile)
        l_ref[rows, :] = lse_tile


def _attn_a(geo, dil, q_arr, q_blk, k_arr, k_blk, v_arr, v_blk):
    m = geo.m
    hb = A_STEP // A_HALF
    last_halo = m // A_HALF - 1
    assert all(((e - s) // dil) % A_STEP == 0 for s, e in zip(geo.starts, geo.ends))

    def main(blk):
        return pl.BlockSpec((A_STEP, A_WIDTH), lambda t: (t, blk))

    def prev(blk):
        return pl.BlockSpec((A_HALF, A_WIDTH), lambda t: (jnp.maximum(t * hb - 1, 0), blk))

    def nxt(blk):
        return pl.BlockSpec((A_HALF, A_WIDTH), lambda t: (jnp.minimum(t * hb + hb, last_halo), blk))

    return pl.pallas_call(
        functools.partial(_attn_a_kernel, dil=dil, geo=geo),
        grid=(m // A_STEP,),
        in_specs=[main(q_blk), prev(k_blk), main(k_blk), nxt(k_blk),
                  prev(v_blk), main(v_blk), nxt(v_blk)],
        out_specs=[pl.BlockSpec((A_STEP, A_WIDTH), lambda t: (t, 0)),
                   pl.BlockSpec((A_STEP, HEAD_DIM), lambda t: (t, 0))],
        out_shape=[jax.ShapeDtypeStruct((m, A_WIDTH), BF16),
                   jax.ShapeDtypeStruct((m, HEAD_DIM), F32)],
        compiler_params=_params(1),
        name=f"attn_a_d{dil}",
    )(q_arr, k_arr, k_arr, k_arr, v_arr, v_arr, v_arr)


COMBINE_TM = 256


def _combine_a_kernel(o0_ref, l0_ref, o1_ref, l1_ref, o2_ref, l2_ref, y_ref, l1_s, l2_s):
    outs = [o0_ref[...].astype(F32)]
    for o_ref, l_ref, l_s, dil in ((o1_ref, l1_ref, l1_s, DIL_GROUPS[1][1]), (o2_ref, l2_ref, l2_s, DIL_GROUPS[2][1])):
        per = COMBINE_TM // dil
        for r in range(dil):
            l_s[pl.ds(r, per, stride=dil), :] = l_ref[r]
        planes = o_ref[...].reshape(COMBINE_TM, A_WIDTH)
        outs.append(jnp.dot(_plane_permutation(COMBINE_TM, per), planes, preferred_element_type=F32))
    a0, a1, a2 = l0_ref[...], l1_s[...], l2_s[...]
    mx = jnp.maximum(jnp.maximum(a0, a1), a2)
    e0, e1, e2 = jnp.exp(a0 - mx), jnp.exp(a1 - mx), jnp.exp(a2 - mx)
    inv = 1.0 / (e0 + e1 + e2)
    weights = (e0 * inv, e1 * inv, e2 * inv)
    for h in range(A_HEADS):
        sl = slice(h * HEAD_DIM, (h + 1) * HEAD_DIM)
        y = sum(jnp.broadcast_to(w[:, h:h + 1], (COMBINE_TM, HEAD_DIM)) * o[:, sl] for w, o in zip(weights, outs))
        y_ref[:, sl] = y.astype(y_ref.dtype)


def _combine_a(outs, lses):
    m = outs[0].shape[0]
    tm = COMBINE_TM
    d1, d2 = DIL_GROUPS[1][1], DIL_GROUPS[2][1]

    def nat(width):
        return pl.BlockSpec((tm, width), lambda i: (i, 0))

    def planes(dil, width):
        return pl.BlockSpec((dil, tm // dil, width), lambda i: (0, i, 0))

    lse_nat = pltpu.VMEM((tm, HEAD_DIM), F32)
    return pl.pallas_call(
        _combine_a_kernel,
        grid=(m // tm,),
        in_specs=[nat(A_WIDTH), nat(HEAD_DIM), planes(d1, A_WIDTH), planes(d1, HEAD_DIM),
                  planes(d2, A_WIDTH), planes(d2, HEAD_DIM)],
        out_specs=nat(A_WIDTH),
        out_shape=jax.ShapeDtypeStruct((m, A_WIDTH), BF16),
        scratch_shapes=[lse_nat, lse_nat],
        compiler_params=_params(1),
        name="combine_a",
    )(outs[0], lses[0],
      outs[1].reshape(d1, m // d1, A_WIDTH), lses[1].reshape(d1, m // d1, HEAD_DIM),
      outs[2].reshape(d2, m // d2, A_WIDTH), lses[2].reshape(d2, m // d2, HEAD_DIM))


B_TQ = 128


def _attn_b_kernel(sink_ref, q_ref, kp_ref, kc_ref, kn_ref, vp_ref, vc_ref, vn_ref, o_ref, *, geo):
    u0 = pl.program_id(0) * B_TQ
    lo, hi = _seq_bounds(geo, u0)
    nk = 3 * B_TQ
    valid = _band_mask(B_TQ, nk, B_TQ, B_WINDOW, lo, hi, u0, stacked=B_GROUP)
    k = jnp.concatenate([kp_ref[...], kc_ref[...], kn_ref[...]], axis=0)
    v = jnp.concatenate([vp_ref[...], vc_ref[...], vn_ref[...]], axis=0)
    row_head = lax.broadcasted_iota(jnp.int32, (B_GROUP * B_TQ, 1), 0) >> (B_TQ.bit_length() - 1)
    for g in range(B_KV_HEADS):
        heads = [slice((g * B_GROUP + h) * HEAD_DIM, (g * B_GROUP + h + 1) * HEAD_DIM) for h in range(B_GROUP)]
        gl = slice(g * HEAD_DIM, (g + 1) * HEAD_DIM)
        q = jnp.concatenate([q_ref[:, sl] for sl in heads], axis=0)
        sk = jnp.zeros((B_GROUP * B_TQ, 1), F32)
        for h in range(B_GROUP):
            sk = jnp.where(row_head == h, sink_ref[g * B_GROUP + h], sk)
        s = lax.dot_general(q, k[:, gl], (((1,), (1,)), ((), ())), preferred_element_type=F32) * SCALE
        s = jnp.where(valid, s, NEG_INF)
        m = jnp.maximum(jnp.max(s, axis=-1, keepdims=True), sk)
        p = jnp.exp(s - m)
        den = jnp.sum(p, axis=-1, keepdims=True) + jnp.exp(sk - m)
        o = jnp.dot(p.astype(BF16), v[:, gl], preferred_element_type=F32) * (1.0 / den)
        for h, sl in enumerate(heads):
            o_ref[:, sl] = o[h * B_TQ:(h + 1) * B_TQ].astype(o_ref.dtype)


def _attn_b(geo, sink, q_arr, q_blk, k_arr, k_blk, v_arr, v_blk):
    m = geo.m
    n_tiles = m // B_TQ

    def kv(blk, shift):
        return pl.BlockSpec((B_TQ, B_KV_WIDTH), lambda t: (jnp.clip(t + shift, 0, n_tiles - 1), blk))

    return pl.pallas_call(
        functools.partial(_attn_b_kernel, geo=geo),
        grid=(n_tiles,),
        in_specs=[pl.BlockSpec(memory_space=pltpu.SMEM),
                  pl.BlockSpec((B_TQ, B_Q_WIDTH), lambda t: (t, q_blk)),
                  kv(k_blk, -1), kv(k_blk, 0), kv(k_blk, 1),
                  kv(v_blk, -1), kv(v_blk, 0), kv(v_blk, 1)],
        out_specs=pl.BlockSpec((B_TQ, B_Q_WIDTH), lambda t: (t, 0)),
        out_shape=jax.ShapeDtypeStruct((m, B_Q_WIDTH), BF16),
        compiler_params=_params(1),
        name="attn_b",
    )(sink, q_arr, k_arr, k_arr, k_arr, v_arr, v_arr, v_arr)


def _merge_kernel(ya_ref, yb_ref, wa_ref, wb_ref, ga_ref, gb_ref, o_ref):
    a = jnp.dot(ya_ref[...], wa_ref[...], preferred_element_type=F32)
    b = jnp.dot(yb_ref[...], wb_ref[...], preferred_element_type=F32)
    o_ref[...] = (ga_ref[...].astype(F32) * a + gb_ref[...].astype(F32) * b).astype(o_ref.dtype)


def _merge(ya, yb, wa, wb, gates, tm=1024, tn=1024):
    m = ya.shape[0]
    n = wa.shape[1]
    return pl.pallas_call(
        _merge_kernel,
        grid=(m // tm, n // tn),
        in_specs=[pl.BlockSpec((tm, ya.shape[1]), lambda i, j: (i, 0)),
                  pl.BlockSpec((tm, yb.shape[1]), lambda i, j: (i, 0)),
                  pl.BlockSpec((wa.shape[0], tn), lambda i, j: (0, j)),
                  pl.BlockSpec((wb.shape[0], tn), lambda i, j: (0, j)),
                  pl.BlockSpec((tm, tn), lambda i, j: (i, j)),
                  pl.BlockSpec((tm, tn), lambda i, j: (i, n // tn + j))],
        out_specs=pl.BlockSpec((tm, tn), lambda i, j: (i, j)),
        out_shape=jax.ShapeDtypeStruct((m, n), BF16),
        compiler_params=_params(2),
        name="merge",
    )(ya, yb, wa, wb, gates, gates)


def _rope_tables(geo, dils):
    half = ROT_HALF
    pos = jnp.concatenate([jnp.arange(e - s) for s, e in zip(geo.starts, geo.ends)])
    inv = ROPE_THETA ** (-jnp.arange(half, dtype=F32) / half)
    ang = pos.astype(F32)[:, None] * inv[None, :]
    cos_sin = jnp.concatenate([jnp.cos(ang), jnp.sin(ang)], axis=1)
    gap = jnp.zeros((geo.m, PARTNER_LANE - half), F32)
    tables = []
    for dil in dils:
        cs = cos_sin.reshape(geo.m // dil, dil, 2 * half).transpose(1, 0, 2).reshape(geo.m, 2 * half)
        cos, sin = cs[:, :half], cs[:, half:]
        cos_t = jnp.concatenate([cos, gap + 1.0, cos, gap + 1.0], axis=1)
        sin_t = jnp.concatenate([-sin, gap, sin, gap], axis=1)
        tables.append((cos_t, sin_t))
    return tables


def _mixers(geo, h, g_mix, wb, sink, later_a, later_b):
    n_nat, n_d4, n_d16 = _rmsnorm_planes(h, g_mix)
    sources = (n_nat, n_d4, n_d16)
    tables = _rope_tables(geo, [dil for _, dil in DIL_GROUPS])

    qa_off, ka_off, va_off, qb_off = 0, 3 * A_WIDTH, 6 * A_WIDTH, 9 * A_WIDTH
    kb_off = qb_off + B_Q_WIDTH
    vb_off = kb_off + B_KV_WIDTH
    gate_off = vb_off + B_KV_WIDTH

    def blocks(start, width, tn):
        assert start % tn == 0 and width % tn == 0
        return tuple(range(start // tn, (start + width) // tn))

    def grp(off, gi, tn):
        return blocks(off + gi * A_WIDTH, A_WIDTH, tn)

    pt = PAIR_TN
    w_rope = _pair_rotary_lanes(wb, grp(qa_off, 1, pt) + grp(ka_off, 1, pt) + grp(qa_off, 2, pt) + grp(ka_off, 2, pt)
                                + grp(qa_off, 0, pt) + grp(ka_off, 0, pt)
                                + blocks(qb_off, B_Q_WIDTH, pt) + blocks(kb_off, B_KV_WIDTH, pt))

    tn0 = B_KV_WIDTH
    rope0, later_b = _proj(n_nat, w_rope, blocks(4 * A_WIDTH, 2 * A_WIDTH + B_Q_WIDTH + B_KV_WIDTH, tn0), "rope", tn0,
                           tables[0], cast=later_b)
    plain0 = _proj(n_nat, wb, grp(va_off, 0, tn0) + blocks(vb_off, B_KV_WIDTH, tn0), "plain", tn0)
    gates, later_a = _proj(n_nat, wb, blocks(gate_off, 2 * D_MODEL, 1024), "gate", 1024, cast=later_a)

    outs, lses = [], []
    o, l = _attn_a(geo, 1, rope0, 0, rope0, 1, plain0, 0)
    outs.append(o)
    lses.append(l)
    for gi in (1, 2):
        dil = DIL_GROUPS[gi][1]
        rope = _proj(sources[gi], w_rope, blocks((gi - 1) * 2 * A_WIDTH, 2 * A_WIDTH, A_WIDTH), "rope", A_WIDTH, tables[gi])
        plain = _proj(sources[gi], wb, grp(va_off, gi, A_WIDTH), "plain", A_WIDTH)
        o, l = _attn_a(geo, dil, rope, 0, rope, 1, plain, 0)
        outs.append(o)
        lses.append(l)
    ya = _combine_a(outs, lses)

    yb = _attn_b(geo, sink, rope0, 2 * A_WIDTH // B_Q_WIDTH, rope0, (2 * A_WIDTH + B_Q_WIDTH) // B_KV_WIDTH,
                 plain0, A_WIDTH // B_KV_WIDTH)
    return ya, yb, gates, later_a, later_b


def _ffn_half_step(parts, total_rows, g, wg, wu, w_down, later_weight=None):
    acts, wd = [], None
    for x in parts:
        act, converted = _ffn_up(_rmsnorm(x, g, BF16), wg, wu, cast=w_down if wd is None else None)
        wd = converted if wd is None else wd
        acts.append(act)
    out, row0, later = None, 0, None
    for x, act in zip(parts, acts):
        last = x is parts[-1]
        out, converted = _mm_residual(act, wd, x, 0.5, tm=1024, tn=512, k_splits=2, out_rows=total_rows,
                                      out_row0=row0, dst=out, cast=later_weight if last else None)
        later = converted if last else later
        row0 += x.shape[0]
    return out, later


def _layer(geo, parts, g_ffn1, w1_gate, w1_up, w1_down, g_mix, w_in, sink_b, w_branch_a, w_branch_b,
           w_out, g_ffn2, w2_gate, w2_up, w2_down, g_final):
    h, w_in_bf16 = _ffn_half_step(parts, geo.m, g_ffn1, w1_gate.astype(BF16), w1_up.astype(BF16), w1_down,
                                  later_weight=w_in)
    ya, yb, gates, wg2, wu2 = _mixers(geo, h, g_mix, w_in_bf16, sink_b, w2_gate, w2_up)
    merged = _merge(ya, yb, w_branch_a.astype(BF16), w_branch_b.astype(BF16), gates)
    h2, _ = _mm_residual(merged, w_out.astype(BF16), h, 1.0, tm=1024, tn=1024)
    x2, _ = _ffn_half_step([h2], geo.m, g_ffn2, wg2, wu2, w2_down)
    outs, row0 = [], 0
    for x in parts:
        outs.append(_rmsnorm(x2, g_final, F32, row0=row0, rows=x.shape[0]))
        row0 += x.shape[0]
    return outs


def kernel(x_prompt, x_sample, g_ffn1, w1_gate, w1_up, w1_down, g_mix, w_in, sink_b, w_branch_a, w_branch_b, w_out, g_ffn2, w2_gate, w2_up, w2_down, g_final):
    assert w1_gate.shape[0] == 1, "single layer"
    bp, sp, _ = x_prompt.shape
    bs, ss, _ = x_sample.shape
    lens = (sp,) * bp + (ss,) * bs
    ends = tuple(sum(lens[:i + 1]) for i in range(len(lens)))
    geo = Geometry(starts=(0,) + ends[:-1], ends=ends)
    parts = [x_prompt.reshape(-1, D_MODEL), x_sample.reshape(-1, D_MODEL)]
    y_prompt, y_sample = _layer(
        geo, parts, g_ffn1[0], w1_gate[0], w1_up[0], w1_down[0], g_mix[0], w_in[0], sink_b[0],
        w_branch_a[0], w_branch_b[0], w_out[0], g_ffn2[0], w2_gate[0], w2_up[0], w2_down[0], g_final)
    return (y_prompt.reshape(x_prompt.shape), y_sample.reshape(x_sample.shape))
```

```python
import functools
import math
from typing import NamedTuple

import jax
import jax.numpy as jnp
from jax import lax
from jax.experimental import pallas as pl
from jax.experimental.pallas import tpu as pltpu

F32 = jnp.float32
BF16 = jnp.bfloat16

D_MODEL = 4096
D_FF = 11008
HEAD_DIM = 128
ROT_DIM = HEAD_DIM // 4
ROPE_THETA = 500000.0
DIL_GROUPS = ((128, 1), (512, 4), (2048, 16))
A_HEADS = 8
A_WIDTH = A_HEADS * HEAD_DIM
A_HALF = 64
B_Q_HEADS = 16
B_KV_HEADS = 4
B_GROUP = B_Q_HEADS // B_KV_HEADS
B_Q_WIDTH = B_Q_HEADS * HEAD_DIM
B_KV_WIDTH = B_KV_HEADS * HEAD_DIM
B_WINDOW = 128
EPS = 1e-6
NEG_INF = -1e30
SCALE = 1.0 / math.sqrt(HEAD_DIM)
LOG2E = math.log2(math.e)
assert all((win // 2) // dil == A_HALF for win, dil in DIL_GROUPS)

VMEM_LIMIT = 58 * 1024 * 1024


class Geometry(NamedTuple):
    starts: tuple
    ends: tuple

    @property
    def m(self):
        return self.ends[-1]


def _params(n_grid):
    return pltpu.CompilerParams(
        dimension_semantics=("arbitrary",) * n_grid,
        vmem_limit_bytes=VMEM_LIMIT)


def _normalise(x, g):
    var = jnp.mean(x * x, axis=-1, keepdims=True)
    return (x * lax.rsqrt(var + EPS)) * g


def _rmsnorm_kernel(x_ref, g_ref, o_ref):
    o_ref[...] = _normalise(x_ref[...], g_ref[...]).astype(o_ref.dtype)


def _rmsnorm(x, g, out_dtype, row0=0, rows=None, tm=256):
    d = x.shape[1]
    rows = x.shape[0] if rows is None else rows
    blk0 = row0 // tm
    assert row0 % tm == 0 and rows % tm == 0
    return pl.pallas_call(
        _rmsnorm_kernel,
        grid=(rows // tm,),
        in_specs=[pl.BlockSpec((tm, d), lambda i: (i + blk0, 0)),
                  pl.BlockSpec((1, d), lambda i: (0, 0))],
        out_specs=pl.BlockSpec((tm, d), lambda i: (i, 0)),
        out_shape=jax.ShapeDtypeStruct((rows, d), out_dtype),
        compiler_params=_params(1),
        name="rmsnorm",
    )(x, g.reshape(1, d))


NORM_TM = 256


def _plane_permutation(tm, dil):
    per = tm // dil
    assert per & (per - 1) == 0, "rows per plane must be a power of two"
    i = lax.broadcasted_iota(jnp.int32, (tm, tm), 0)
    j = lax.broadcasted_iota(jnp.int32, (tm, tm), 1)
    src = (i & (per - 1)) * dil + (i >> (per.bit_length() - 1))
    return (j == src).astype(BF16)


def _rmsnorm_planes_kernel(x_ref, g_ref, n_ref, n4_ref, n16_ref):
    y = _normalise(x_ref[...], g_ref[...]).astype(BF16)
    n_ref[...] = y
    for ref, dil in ((n4_ref, DIL_GROUPS[1][1]), (n16_ref, DIL_GROUPS[2][1])):
        perm = jnp.dot(_plane_permutation(NORM_TM, dil), y, preferred_element_type=F32).astype(BF16)
        ref[...] = perm.reshape(dil, NORM_TM // dil, y.shape[1])


def _rmsnorm_planes(x, g):
    m, d = x.shape
    tm = NORM_TM
    d4, d16 = DIL_GROUPS[1][1], DIL_GROUPS[2][1]
    n, n4, n16 = pl.pallas_call(
        _rmsnorm_planes_kernel,
        grid=(m // tm,),
        in_specs=[pl.BlockSpec((tm, d), lambda i: (i, 0)),
                  pl.BlockSpec((1, d), lambda i: (0, 0))],
        out_specs=[pl.BlockSpec((tm, d), lambda i: (i, 0)),
                   pl.BlockSpec((d4, tm // d4, d), lambda i: (0, i, 0)),
                   pl.BlockSpec((d16, tm // d16, d), lambda i: (0, i, 0))],
        out_shape=[jax.ShapeDtypeStruct((m, d), BF16),
                   jax.ShapeDtypeStruct((d4, m // d4, d), BF16),
                   jax.ShapeDtypeStruct((d16, m // d16, d), BF16)],
        compiler_params=_params(1),
        name="rmsnorm_planes",
    )(x, g.reshape(1, d))
    return n, n4.reshape(m, d), n16.reshape(m, d)


def _sigmoid(x):
    return 0.5 * jnp.tanh(0.5 * x) + 0.5


def _hosted_cast_specs(w, grid, col_tile=None):
    rows, cols = w.shape
    steps = math.prod(grid)
    rb = next(b for b in range(16, rows + 1, 16) if rows % b == 0 and rows // b <= steps)
    last = rows // rb - 1

    def step(*ids):
        flat = ids[0]
        for extent, i in zip(grid[1:], ids[1:len(grid)]):
            flat = flat * extent + i
        return jnp.minimum(flat, last)

    in_spec = pl.BlockSpec((rb, cols), lambda *ids: (step(*ids), 0))
    if col_tile is None:
        return in_spec, in_spec, jax.ShapeDtypeStruct((rows, cols), BF16)
    n_tiles = cols // col_tile
    out_spec = pl.BlockSpec((n_tiles, rb, col_tile), lambda *ids: (0, step(*ids), 0))
    return in_spec, out_spec, jax.ShapeDtypeStruct((n_tiles, rows, col_tile), BF16)


def _hosting_cast(body, n_lead):
    def kernel(*refs):
        src, dst = refs[n_lead], refs[-1]
        if len(dst.shape) == 2:
            dst[...] = src[...].astype(dst.dtype)
        else:
            tile = dst.shape[2]
            for c in range(dst.shape[0]):
                dst[c] = src[:, c * tile:(c + 1) * tile].astype(dst.dtype)
        body(*refs[:n_lead], *refs[n_lead + 1:-1])
    return kernel


def _ffn_up_kernel(x_ref, wg_ref, wu_ref, *rest):
    o_ref = rest[-1]
    x = x_ref[...]
    g = jnp.dot(x, wg_ref[...], preferred_element_type=F32)
    u = jnp.dot(x, wu_ref[...], preferred_element_type=F32)
    o_ref[...] = (g * _sigmoid(g) * u).astype(o_ref.dtype)


FFN_TM = 1024
FFN_TN = 512
DOWN_TN = 512
FFN_TAIL = D_FF % FFN_TN
assert FFN_TAIL % HEAD_DIM == 0 and (D_FF - FFN_TAIL) % FFN_TAIL == 0


def _ffn_up(xn, wg, wu, cast=None, cast_col_tile=None):
    m, k = xn.shape
    n = wg.shape[1]
    tm = FFN_TM
    grid = (m // tm, n // FFN_TN)
    x_spec = pl.BlockSpec((tm, k), lambda i, j: (i, 0))
    body, args = _ffn_up_kernel, [xn, wg, wu]
    in_specs = [x_spec,
                pl.BlockSpec((k, FFN_TN), lambda i, j: (0, j)),
                pl.BlockSpec((k, FFN_TN), lambda i, j: (0, j))]
    out_specs = [pl.BlockSpec((tm, FFN_TN), lambda i, j: (i, j))]
    out_shape = [jax.ShapeDtypeStruct((m, n), BF16)]
    if cast is not None:
        body = _hosting_cast(body, len(args))
        args.append(cast)
        cast_in, cast_out, cast_shape = _hosted_cast_specs(cast, grid, cast_col_tile)
        in_specs.append(cast_in)
        out_specs.append(cast_out)
        out_shape.append(cast_shape)
    act, *converted = pl.pallas_call(
        body,
        grid=grid,
        in_specs=in_specs,
        out_specs=out_specs,
        out_shape=out_shape,
        compiler_params=_params(2),
        name="ffn_up",
    )(*args)
    tail_block = (n - FFN_TAIL) // FFN_TAIL
    w_tail = pl.BlockSpec((k, FFN_TAIL), lambda i, j: (0, tail_block))
    act = pl.pallas_call(
        _ffn_up_kernel,
        grid=(m // tm, 1),
        in_specs=[x_spec, w_tail, w_tail, pl.BlockSpec(memory_space=pl.ANY)],
        out_specs=pl.BlockSpec((tm, FFN_TAIL), lambda i, j: (i, tail_block)),
        out_shape=jax.ShapeDtypeStruct((m, n), BF16),
        input_output_aliases={3: 0},
        compiler_params=_params(2),
        name="ffn_up_tail",
    )(xn, wg, wu, act)
    return act, (converted[0] if converted else None)


def _mm_residual_kernel(x_ref, w_ref, r_ref, *rest, scale):
    o_ref = rest[-1]
    acc = jnp.dot(x_ref[...], w_ref[...], preferred_element_type=F32)
    o_ref[...] = r_ref[...] + scale * acc


def _mm_residual(x, w, res, scale, tm, tn, k_splits=1, out_rows=None, out_row0=0, dst=None, cast=None):
    m, k = x.shape
    tiled = len(w.shape) == 3
    n = w.shape[0] * w.shape[2] if tiled else w.shape[1]
    assert not tiled or w.shape[2] == tn
    out_rows = m if out_rows is None else out_rows
    blk0 = out_row0 // tm
    tk = k // k_splits
    grid = (m // tm, n // tn)
    assert out_row0 % tm == 0 and k % k_splits == 0 and tk % HEAD_DIM == 0
    converted = None
    for s in range(k_splits):
        first = s == 0
        body = functools.partial(_mm_residual_kernel, scale=scale)
        args = [x, w, res if first else dst]
        w_spec = (pl.BlockSpec((None, tk, tn), lambda i, j, s=s: (j, s, 0)) if tiled
                  else pl.BlockSpec((tk, tn), lambda i, j, s=s: (s, j)))
        in_specs = [pl.BlockSpec((tm, tk), lambda i, j, s=s: (i, s)),
                    w_spec,
                    pl.BlockSpec((tm, tn), (lambda i, j: (i, j)) if first else (lambda i, j: (i + blk0, j)))]
        out_specs = [pl.BlockSpec((tm, tn), lambda i, j: (i + blk0, j))]
        out_shape = [jax.ShapeDtypeStruct((out_rows, n), F32)]
        aliases = {}
        if first and dst is not None:
            args.append(dst)
            in_specs.append(pl.BlockSpec(memory_space=pl.ANY))
            aliases = {3: 0}
        elif not first:
            aliases = {2: 0}
        if first and cast is not None:
            body = _hosting_cast(body, len(args))
            args.append(cast)
            cast_in, cast_out, cast_shape = _hosted_cast_specs(cast, grid)
            in_specs.append(cast_in)
            out_specs.append(cast_out)
            out_shape.append(cast_shape)
        dst, *extra = pl.pallas_call(
            body,
            grid=grid,
            in_specs=in_specs,
            out_specs=out_specs,
            out_shape=out_shape,
            input_output_aliases=aliases,
            compiler_params=_params(2),
            name="mm_residual",
        )(*args)
        converted = extra[0] if extra else converted
    return dst, converted


ROT_HALF = ROT_DIM // 2
PARTNER_LANE = HEAD_DIM // 2


PAIR_TN = 512
ROPE0_TN = 768


def _pair_rotary_lanes_kernel(tbl_ref, w_ref, o_ref):
    src = lax.broadcasted_iota(jnp.int32, (PAIR_TN, PAIR_TN), 0)
    dst = lax.broadcasted_iota(jnp.int32, (PAIR_TN, PAIR_TN), 1)
    lane = dst & (HEAD_DIM - 1)
    shift = PARTNER_LANE - ROT_HALF
    want = dst + jnp.where((lane >= ROT_HALF) & (lane < ROT_DIM), shift, 0) \
               - jnp.where((lane >= PARTNER_LANE) & (lane < PARTNER_LANE + ROT_HALF), shift, 0)
    perm = (src == want).astype(BF16)
    o_ref[...] = jnp.dot(w_ref[...], perm, preferred_element_type=F32).astype(o_ref.dtype)


def _pair_rotary_lanes(w, col_blocks):
    k = w.shape[0]
    grid_spec = pltpu.PrefetchScalarGridSpec(
        num_scalar_prefetch=1,
        grid=(len(col_blocks),),
        in_specs=[pl.BlockSpec((k, PAIR_TN), lambda j, tbl: (0, tbl[j]))],
        out_specs=pl.BlockSpec((k, PAIR_TN), lambda j, tbl: (0, j)))
    return pl.pallas_call(
        _pair_rotary_lanes_kernel,
        grid_spec=grid_spec,
        out_shape=jax.ShapeDtypeStruct((k, len(col_blocks) * PAIR_TN), w.dtype),
        compiler_params=_params(1),
        name="pair_rotary_lanes",
    )(jnp.asarray(col_blocks, jnp.int32), w)


def _proj_rope_kernel(tbl_ref, x_ref, w_ref, cos_ref, sin_ref, o_ref):
    acc = jnp.dot(x_ref[...], w_ref[...], preferred_element_type=F32)
    cos, sin = cos_ref[...], sin_ref[...]
    for h in range(acc.shape[1] // HEAD_DIM):
        sl = slice(h * HEAD_DIM, (h + 1) * HEAD_DIM)
        seg = acc[:, sl]
        o_ref[:, sl] = (seg * cos + pltpu.roll(seg, PARTNER_LANE, axis=1) * sin).astype(o_ref.dtype)


def _proj_plain_kernel(tbl_ref, x_ref, w_ref, o_ref):
    o_ref[...] = jnp.dot(x_ref[...], w_ref[...], preferred_element_type=F32).astype(o_ref.dtype)


def _proj_gate_kernel(tbl_ref, x_ref, w_ref, o_ref):
    acc = jnp.dot(x_ref[...], w_ref[...], preferred_element_type=F32)
    o_ref[...] = _sigmoid(acc).astype(o_ref.dtype)


def _proj(xn, w, col_blocks, kind, tn, tables=(), tm=1024, cast=None):
    m, k = xn.shape
    body = {"rope": _proj_rope_kernel, "plain": _proj_plain_kernel, "gate": _proj_gate_kernel}[kind]
    grid = (m // tm, len(col_blocks))
    tab = pl.BlockSpec((tm, HEAD_DIM), lambda i, j, tbl: (i, 0))
    args = [jnp.asarray(col_blocks, jnp.int32), xn, w, *tables]
    in_specs = [pl.BlockSpec((tm, k), lambda i, j, tbl: (i, 0)),
                pl.BlockSpec((k, tn), lambda i, j, tbl: (0, tbl[j]))] + [tab] * len(tables)
    out_specs = [pl.BlockSpec((tm, tn), lambda i, j, tbl: (i, j))]
    out_shape = [jax.ShapeDtypeStruct((m, len(col_blocks) * tn), BF16)]
    if cast is not None:
        body = _hosting_cast(body, len(args))
        args.append(cast)
        cast_in, cast_out, cast_shape = _hosted_cast_specs(cast, grid)
        in_specs.append(cast_in)
        out_specs.append(cast_out)
        out_shape.append(cast_shape)
    outs = pl.pallas_call(
        body,
        grid_spec=pltpu.PrefetchScalarGridSpec(num_scalar_prefetch=1, grid=grid, in_specs=in_specs,
                                               out_specs=out_specs),
        out_shape=out_shape,
        compiler_params=_params(2),
        name=f"proj_{kind}",
    )(*args)
    return outs[0] if cast is None else tuple(outs)


def _seq_bounds(geo, g0):
    lo = jnp.int32(geo.starts[-1])
    hi = jnp.int32(geo.ends[-1])
    for s, e in zip(reversed(geo.starts[:-1]), reversed(geo.ends[:-1])):
        lo = jnp.where(g0 < e, s, lo)
        hi = jnp.where(g0 < e, e, hi)
    return lo, hi


def _band_mask(tq, nk, halo, half_window, lo, hi, u0, stacked=1):
    assert tq & (tq - 1) == 0
    ii = lax.broadcasted_iota(jnp.int32, (stacked * tq, nk), 0) & (tq - 1)
    jj = lax.broadcasted_iota(jnp.int32, (stacked * tq, nk), 1)
    rel = jj - ii - halo
    return ((rel >= -half_window) & (rel <= half_window)
            & (jj >= lo - u0 + halo) & (jj < hi - u0 + halo))


def _mask_bias(valid):
    return jnp.where(valid, 0.0, NEG_INF).astype(F32)


A_TQ = 128
A_STEP = 256


def _attn_a_kernel(q_ref, kp_ref, kc_ref, kn_ref, vp_ref, vc_ref, vn_ref, o_ref, l_ref, *, dil, geo):
    step0 = pl.program_id(0) * A_STEP
    plane_rows = geo.m // dil
    base = (step0 // plane_rows) * plane_rows
    lo, hi = _seq_bounds(geo, (step0 - base) * dil)
    lo, hi = base + lo // dil, base + hi // dil
    nk = A_TQ + 2 * A_HALF
    k = jnp.concatenate([kp_ref[...], kc_ref[...], kn_ref[...]], axis=0)
    v = jnp.concatenate([vp_ref[...], vc_ref[...], vn_ref[...]], axis=0)
    lane = lax.broadcasted_iota(jnp.int32, (A_TQ, HEAD_DIM), 1)
    for sub in range(A_STEP // A_TQ):
        rows = slice(sub * A_TQ, (sub + 1) * A_TQ)
        keys = slice(sub * A_TQ, sub * A_TQ + nk)
        bias = _mask_bias(_band_mask(A_TQ, nk, A_HALF, A_HALF, lo, hi, step0 + sub * A_TQ))
        lse_tile = jnp.zeros((A_TQ, HEAD_DIM), F32)
        for h in range(A_HEADS):
            sl = slice(h * HEAD_DIM, (h + 1) * HEAD_DIM)
            s = lax.dot_general(q_ref[rows, sl], k[keys, sl], (((1,), (1,)), ((), ())),
                                preferred_element_type=F32) * (SCALE * LOG2E) + bias
            m = jnp.max(s, axis=-1, keepdims=True)
            p = jnp.exp2(s - m)
            den = jnp.sum(p, axis=-1, keepdims=True)
            o = jnp.dot(p.astype(BF16), v[keys, sl], preferred_element_type=F32)
            o_ref[rows, sl] = (o * (1.0 / den)).astype(o_ref.dtype)
            lse_tile = jnp.where(lane == h, m * (1.0 / LOG2E) + jnp.log(den), lse_tile)
        l_ref[rows, :] = lse_tile


def _attn_a(geo, dil, q_arr, q_blk, k_arr, k_blk, v_arr, v_blk):
    m = geo.m
    hb = A_STEP // A_HALF
    last_halo = m // A_HALF - 1
    assert all(((e - s) // dil) % A_STEP == 0 for s, e in zip(geo.starts, geo.ends))

    def main(blk):
        return pl.BlockSpec((A_STEP, A_WIDTH), lambda t: (t, blk))

    def prev(blk):
        return pl.BlockSpec((A_HALF, A_WIDTH), lambda t: (jnp.maximum(t * hb - 1, 0), blk))

    def nxt(blk):
        return pl.BlockSpec((A_HALF, A_WIDTH), lambda t: (jnp.minimum(t * hb + hb, last_halo), blk))

    return pl.pallas_call(
        functools.partial(_attn_a_kernel, dil=dil, geo=geo),
        grid=(m // A_STEP,),
        in_specs=[main(q_blk), prev(k_blk), main(k_blk), nxt(k_blk),
                  prev(v_blk), main(v_blk), nxt(v_blk)],
        out_specs=[pl.BlockSpec((A_STEP, A_WIDTH), lambda t: (t, 0)),
                   pl.BlockSpec((A_STEP, HEAD_DIM), lambda t: (t, 0))],
        out_shape=[jax.ShapeDtypeStruct((m, A_WIDTH), BF16),
                   jax.ShapeDtypeStruct((m, HEAD_DIM), F32)],
        compiler_params=_params(1),
        name=f"attn_a_d{dil}",
    )(q_arr, k_arr, k_arr, k_arr, v_arr, v_arr, v_arr)


COMBINE_TM = 256


def _combine_a_kernel(o0_ref, l0_ref, o1_ref, l1_ref, o2_ref, l2_ref, y_ref, l1_s, l2_s):
    outs = [o0_ref[...].astype(F32)]
    for o_ref, l_ref, l_s, dil in ((o1_ref, l1_ref, l1_s, DIL_GROUPS[1][1]), (o2_ref, l2_ref, l2_s, DIL_GROUPS[2][1])):
        per = COMBINE_TM // dil
        for r in range(dil):
            l_s[pl.ds(r, per, stride=dil), :] = l_ref[r]
        planes = o_ref[...].reshape(COMBINE_TM, A_WIDTH)
        outs.append(jnp.dot(_plane_permutation(COMBINE_TM, per), planes, preferred_element_type=F32))
    a0, a1, a2 = l0_ref[...], l1_s[...], l2_s[...]
    mx = jnp.maximum(jnp.maximum(a0, a1), a2)
    e0, e1, e2 = jnp.exp(a0 - mx), jnp.exp(a1 - mx), jnp.exp(a2 - mx)
    inv = 1.0 / (e0 + e1 + e2)
    weights = (e0 * inv, e1 * inv, e2 * inv)
    for h in range(A_HEADS):
        sl = slice(h * HEAD_DIM, (h + 1) * HEAD_DIM)
        y = sum(jnp.broadcast_to(w[:, h:h + 1], (COMBINE_TM, HEAD_DIM)) * o[:, sl] for w, o in zip(weights, outs))
        y_ref[:, sl] = y.astype(y_ref.dtype)


def _combine_a(outs, lses):
    m = outs[0].shape[0]
    tm = COMBINE_TM
    d1, d2 = DIL_GROUPS[1][1], DIL_GROUPS[2][1]

    def nat(width):
        return pl.BlockSpec((tm, width), lambda i: (i, 0))

    def planes(dil, width):
        return pl.BlockSpec((dil, tm // dil, width), lambda i: (0, i, 0))

    lse_nat = pltpu.VMEM((tm, HEAD_DIM), F32)
    return pl.pallas_call(
        _combine_a_kernel,
        grid=(m // tm,),
        in_specs=[nat(A_WIDTH), nat(HEAD_DIM), planes(d1, A_WIDTH), planes(d1, HEAD_DIM),
                  planes(d2, A_WIDTH), planes(d2, HEAD_DIM)],
        out_specs=nat(A_WIDTH),
        out_shape=jax.ShapeDtypeStruct((m, A_WIDTH), BF16),
        scratch_shapes=[lse_nat, lse_nat],
        compiler_params=_params(1),
        name="combine_a",
    )(outs[0], lses[0],
      outs[1].reshape(d1, m // d1, A_WIDTH), lses[1].reshape(d1, m // d1, HEAD_DIM),
      outs[2].reshape(d2, m // d2, A_WIDTH), lses[2].reshape(d2, m // d2, HEAD_DIM))


B_TQ = 128
B_STEP = 256


def _attn_b_kernel(sink_ref, q_ref, kp_ref, kc_ref, kn_ref, vp_ref, vc_ref, vn_ref, o_ref, *, geo):
    step0 = pl.program_id(0) * B_STEP
    lo, hi = _seq_bounds(geo, step0)
    nk = 3 * B_TQ
    k = jnp.concatenate([kp_ref[...], kc_ref[...], kn_ref[...]], axis=0)
    v = jnp.concatenate([vp_ref[...], vc_ref[...], vn_ref[...]], axis=0)
    row_head = lax.broadcasted_iota(jnp.int32, (B_GROUP * B_TQ, 1), 0) >> (B_TQ.bit_length() - 1)
    for sub in range(B_STEP // B_TQ):
        rows = slice(sub * B_TQ, (sub + 1) * B_TQ)
        keys = slice(sub * B_TQ, sub * B_TQ + nk)
        bias = _mask_bias(_band_mask(B_TQ, nk, B_TQ, B_WINDOW, lo, hi, step0 + sub * B_TQ, stacked=B_GROUP))
        for g in range(B_KV_HEADS):
            heads = [slice((g * B_GROUP + h) * HEAD_DIM, (g * B_GROUP + h + 1) * HEAD_DIM) for h in range(B_GROUP)]
            gl = slice(g * HEAD_DIM, (g + 1) * HEAD_DIM)
            q = jnp.concatenate([q_ref[rows, sl] for sl in heads], axis=0)
            sk = jnp.zeros((B_GROUP * B_TQ, 1), F32)
            for h in range(B_GROUP):
                sk = jnp.where(row_head == h, sink_ref[g * B_GROUP + h] * LOG2E, sk)
            s = lax.dot_general(q, k[keys, gl], (((1,), (1,)), ((), ())),
                                preferred_element_type=F32) * (SCALE * LOG2E) + bias
            m = jnp.maximum(jnp.max(s, axis=-1, keepdims=True), sk)
            p = jnp.exp2(s - m)
            den = jnp.sum(p, axis=-1, keepdims=True) + jnp.exp2(sk - m)
            o = jnp.dot(p.astype(BF16), v[keys, gl], preferred_element_type=F32) * (1.0 / den)
            for h, sl in enumerate(heads):
                o_ref[rows, sl] = o[h * B_TQ:(h + 1) * B_TQ].astype(o_ref.dtype)


def _attn_b(geo, sink, q_arr, q_blk, k_arr, k_blk, v_arr, v_blk):
    m = geo.m
    hb = B_STEP // B_TQ
    last_halo = m // B_TQ - 1
    assert all((e - s) % B_STEP == 0 for s, e in zip(geo.starts, geo.ends))

    def main(width, blk):
        return pl.BlockSpec((B_STEP, width), lambda t: (t, blk))

    def prev(blk):
        return pl.BlockSpec((B_TQ, B_KV_WIDTH), lambda t: (jnp.maximum(t * hb - 1, 0), blk))

    def nxt(blk):
        return pl.BlockSpec((B_TQ, B_KV_WIDTH), lambda t: (jnp.minimum(t * hb + hb, last_halo), blk))

    return pl.pallas_call(
        functools.partial(_attn_b_kernel, geo=geo),
        grid=(m // B_STEP,),
        in_specs=[pl.BlockSpec(memory_space=pltpu.SMEM),
                  main(B_Q_WIDTH, q_blk),
                  prev(k_blk), main(B_KV_WIDTH, k_blk), nxt(k_blk),
                  prev(v_blk), main(B_KV_WIDTH, v_blk), nxt(v_blk)],
        out_specs=pl.BlockSpec((B_STEP, B_Q_WIDTH), lambda t: (t, 0)),
        out_shape=jax.ShapeDtypeStruct((m, B_Q_WIDTH), BF16),
        compiler_params=_params(1),
        name="attn_b",
    )(sink, q_arr, k_arr, k_arr, k_arr, v_arr, v_arr, v_arr)


def _merge_kernel(ya_ref, yb_ref, wa_ref, wb_ref, ga_ref, gb_ref, o_ref):
    a = jnp.dot(ya_ref[...], wa_ref[...], preferred_element_type=F32)
    b = jnp.dot(yb_ref[...], wb_ref[...], preferred_element_type=F32)
    o_ref[...] = (ga_ref[...].astype(F32) * a + gb_ref[...].astype(F32) * b).astype(o_ref.dtype)


def _merge(ya, yb, wa, wb, gates, tm=1024, tn=1024):
    m = ya.shape[0]
    n = wa.shape[1]
    return pl.pallas_call(
        _merge_kernel,
        grid=(m // tm, n // tn),
        in_specs=[pl.BlockSpec((tm, ya.shape[1]), lambda i, j: (i, 0)),
                  pl.BlockSpec((tm, yb.shape[1]), lambda i, j: (i, 0)),
                  pl.BlockSpec((wa.shape[0], tn), lambda i, j: (0, j)),
                  pl.BlockSpec((wb.shape[0], tn), lambda i, j: (0, j)),
                  pl.BlockSpec((tm, tn), lambda i, j: (i, j)),
                  pl.BlockSpec((tm, tn), lambda i, j: (i, n // tn + j))],
        out_specs=pl.BlockSpec((tm, tn), lambda i, j: (i, j)),
        out_shape=jax.ShapeDtypeStruct((m, n), BF16),
        compiler_params=_params(2),
        name="merge",
    )(ya, yb, wa, wb, gates, gates)


def _rope_tables(geo, dils):
    half = ROT_HALF
    pos = jnp.concatenate([jnp.arange(e - s) for s, e in zip(geo.starts, geo.ends)])
    inv = ROPE_THETA ** (-jnp.arange(half, dtype=F32) / half)
    ang = pos.astype(F32)[:, None] * inv[None, :]
    cos, sin = jnp.cos(ang), jnp.sin(ang)
    gap = jnp.zeros((geo.m, PARTNER_LANE - half), F32)
    cos_t = jnp.concatenate([cos, gap + 1.0, cos, gap + 1.0], axis=1)
    sin_t = jnp.concatenate([-sin, gap, sin, gap], axis=1)

    def plane_order(t, dil):
        return t.reshape(geo.m // dil, dil, HEAD_DIM).transpose(1, 0, 2).reshape(geo.m, HEAD_DIM)

    return [(plane_order(cos_t, dil), plane_order(sin_t, dil)) for dil in dils]


def _mixers(geo, h, g_mix, wb, sink, later_a, later_b):
    n_nat, n_d4, n_d16 = _rmsnorm_planes(h, g_mix)
    sources = (n_nat, n_d4, n_d16)
    tables = _rope_tables(geo, [dil for _, dil in DIL_GROUPS])

    qa_off, ka_off, va_off, qb_off = 0, 3 * A_WIDTH, 6 * A_WIDTH, 9 * A_WIDTH
    kb_off = qb_off + B_Q_WIDTH
    vb_off = kb_off + B_KV_WIDTH
    gate_off = vb_off + B_KV_WIDTH

    def blocks(start, width, tn):
        assert start % tn == 0 and width % tn == 0
        return tuple(range(start // tn, (start + width) // tn))

    def grp(off, gi, tn):
        return blocks(off + gi * A_WIDTH, A_WIDTH, tn)

    pt = PAIR_TN
    w_rope0 = _pair_rotary_lanes(wb, grp(qa_off, 0, pt) + grp(ka_off, 0, pt)
                                 + blocks(qb_off, B_Q_WIDTH, pt) + blocks(kb_off, B_KV_WIDTH, pt))
    w_rope = _pair_rotary_lanes(wb, grp(qa_off, 1, pt) + grp(ka_off, 1, pt) + grp(qa_off, 2, pt) + grp(ka_off, 2, pt))

    tn0 = B_KV_WIDTH
    rope0, later_b = _proj(n_nat, w_rope0, blocks(0, w_rope0.shape[1], ROPE0_TN), "rope", ROPE0_TN,
                           tables[0], cast=later_b)
    plain0 = _proj(n_nat, wb, grp(va_off, 0, tn0) + blocks(vb_off, B_KV_WIDTH, tn0), "plain", tn0)
    gates, later_a = _proj(n_nat, wb, blocks(gate_off, 2 * D_MODEL, 1024), "gate", 1024, cast=later_a)

    outs, lses = [], []
    o, l = _attn_a(geo, 1, rope0, 0, rope0, 1, plain0, 0)
    outs.append(o)
    lses.append(l)
    for gi in (1, 2):
        dil = DIL_GROUPS[gi][1]
        rope = _proj(sources[gi], w_rope, blocks((gi - 1) * 2 * A_WIDTH, 2 * A_WIDTH, A_WIDTH), "rope", A_WIDTH, tables[gi])
        plain = _proj(sources[gi], wb, grp(va_off, gi, A_WIDTH), "plain", A_WIDTH)
        o, l = _attn_a(geo, dil, rope, 0, rope, 1, plain, 0)
        outs.append(o)
        lses.append(l)
    ya = _combine_a(outs, lses)

    yb = _attn_b(geo, sink, rope0, 2 * A_WIDTH // B_Q_WIDTH, rope0, (2 * A_WIDTH + B_Q_WIDTH) // B_KV_WIDTH,
                 plain0, A_WIDTH // B_KV_WIDTH)
    return ya, yb, gates, later_a, later_b


def _ffn_half_step(parts, total_rows, g, wg, wu, w_down, later_weight=None):
    acts, wd = [], None
    for x in parts:
        act, converted = _ffn_up(_rmsnorm(x, g, BF16), wg, wu, cast=w_down if wd is None else None,
                                 cast_col_tile=DOWN_TN)
        wd = converted if wd is None else wd
        acts.append(act)
    out, row0, later = None, 0, None
    for x, act in zip(parts, acts):
        last = x is parts[-1]
        out, converted = _mm_residual(act, wd, x, 0.5, tm=1024, tn=DOWN_TN, k_splits=2, out_rows=total_rows,
                                      out_row0=row0, dst=out, cast=later_weight if last else None)
        later = converted if last else later
        row0 += x.shape[0]
    return out, later


def _layer(geo, parts, g_ffn1, w1_gate, w1_up, w1_down, g_mix, w_in, sink_b, w_branch_a, w_branch_b,
           w_out, g_ffn2, w2_gate, w2_up, w2_down, g_final):
    h, w_in_bf16 = _ffn_half_step(parts, geo.m, g_ffn1, w1_gate.astype(BF16), w1_up.astype(BF16), w1_down,
                                  later_weight=w_in)
    ya, yb, gates, wg2, wu2 = _mixers(geo, h, g_mix, w_in_bf16, sink_b, w2_gate, w2_up)
    merged = _merge(ya, yb, w_branch_a.astype(BF16), w_branch_b.astype(BF16), gates)
    h2, _ = _mm_residual(merged, w_out.astype(BF16), h, 1.0, tm=1024, tn=1024)
    x2, _ = _ffn_half_step([h2], geo.m, g_ffn2, wg2, wu2, w2_down)
    outs, row0 = [], 0
    for x in parts:
        outs.append(_rmsnorm(x2, g_final, F32, row0=row0, rows=x.shape[0]))
        row0 += x.shape[0]
    return outs


def kernel(x_prompt, x_sample, g_ffn1, w1_gate, w1_up, w1_down, g_mix, w_in, sink_b, w_branch_a, w_branch_b, w_out, g_ffn2, w2_gate, w2_up, w2_down, g_final):
    assert w1_gate.shape[0] == 1, "single layer"
    bp, sp, _ = x_prompt.shape
    bs, ss, _ = x_sample.shape
    lens = (sp,) * bp + (ss,) * bs
    ends = tuple(sum(lens[:i + 1]) for i in range(len(lens)))
    geo = Geometry(starts=(0,) + ends[:-1], ends=ends)
    parts = [x_prompt.reshape(-1, D_MODEL), x_sample.reshape(-1, D_MODEL)]
    y_prompt, y_sample = _layer(
        geo, parts, g_ffn1[0], w1_gate[0], w1_up[0], w1_down[0], g_mix[0], w_in[0], sink_b[0],
        w_branch_a[0], w_branch_b[0], w_out[0], g_ffn2[0], w2_gate[0], w2_up[0], w2_down[0], g_final)
    return (y_prompt.reshape(x_prompt.shape), y_sample.reshape(x_sample.shape))
```

```python
import functools
import math
from typing import NamedTuple

import jax
import jax.numpy as jnp
from jax import lax
from jax.experimental import pallas as pl
from jax.experimental.pallas import tpu as pltpu

F32 = jnp.float32
BF16 = jnp.bfloat16

D_MODEL = 4096
D_FF = 11008
HEAD_DIM = 128
ROT_DIM = HEAD_DIM // 4
ROPE_THETA = 500000.0
DIL_GROUPS = ((128, 1), (512, 4), (2048, 16))
A_HEADS = 8
A_WIDTH = A_HEADS * HEAD_DIM
A_HALF = 64
B_Q_HEADS = 16
B_KV_HEADS = 4
B_GROUP = B_Q_HEADS // B_KV_HEADS
B_Q_WIDTH = B_Q_HEADS * HEAD_DIM
B_KV_WIDTH = B_KV_HEADS * HEAD_DIM
B_WINDOW = 128
EPS = 1e-6
NEG_INF = -1e30
SCALE = 1.0 / math.sqrt(HEAD_DIM)
LOG2E = math.log2(math.e)
assert all((win // 2) // dil == A_HALF for win, dil in DIL_GROUPS)

VMEM_LIMIT = 58 * 1024 * 1024


class Geometry(NamedTuple):
    starts: tuple
    ends: tuple

    @property
    def m(self):
        return self.ends[-1]


def _params(n_grid):
    return pltpu.CompilerParams(
        dimension_semantics=("arbitrary",) * n_grid,
        vmem_limit_bytes=VMEM_LIMIT)


def _normalise(x, g):
    var = jnp.mean(x * x, axis=-1, keepdims=True)
    return (x * lax.rsqrt(var + EPS)) * g


def _rmsnorm_kernel(x_ref, g_ref, o_ref):
    o_ref[...] = _normalise(x_ref[...], g_ref[...]).astype(o_ref.dtype)


def _rmsnorm(x, g, out_dtype, row0=0, rows=None, tm=256):
    d = x.shape[1]
    rows = x.shape[0] if rows is None else rows
    blk0 = row0 // tm
    assert row0 % tm == 0 and rows % tm == 0
    return pl.pallas_call(
        _rmsnorm_kernel,
        grid=(rows // tm,),
        in_specs=[pl.BlockSpec((tm, d), lambda i: (i + blk0, 0)),
                  pl.BlockSpec((1, d), lambda i: (0, 0))],
        out_specs=pl.BlockSpec((tm, d), lambda i: (i, 0)),
        out_shape=jax.ShapeDtypeStruct((rows, d), out_dtype),
        compiler_params=_params(1),
        name="rmsnorm",
    )(x, g.reshape(1, d))


NORM_TM = 256


def _plane_permutation(tm, dil):
    per = tm // dil
    assert per & (per - 1) == 0, "rows per plane must be a power of two"
    i = lax.broadcasted_iota(jnp.int32, (tm, tm), 0)
    j = lax.broadcasted_iota(jnp.int32, (tm, tm), 1)
    src = (i & (per - 1)) * dil + (i >> (per.bit_length() - 1))
    return (j == src).astype(BF16)


def _rmsnorm_planes_kernel(x_ref, g_ref, n_ref, n4_ref, n16_ref):
    y = _normalise(x_ref[...], g_ref[...]).astype(BF16)
    n_ref[...] = y
    for ref, dil in ((n4_ref, DIL_GROUPS[1][1]), (n16_ref, DIL_GROUPS[2][1])):
        perm = jnp.dot(_plane_permutation(NORM_TM, dil), y, preferred_element_type=F32).astype(BF16)
        ref[...] = perm.reshape(dil, NORM_TM // dil, y.shape[1])


def _rmsnorm_planes(x, g):
    m, d = x.shape
    tm = NORM_TM
    d4, d16 = DIL_GROUPS[1][1], DIL_GROUPS[2][1]
    n, n4, n16 = pl.pallas_call(
        _rmsnorm_planes_kernel,
        grid=(m // tm,),
        in_specs=[pl.BlockSpec((tm, d), lambda i: (i, 0)),
                  pl.BlockSpec((1, d), lambda i: (0, 0))],
        out_specs=[pl.BlockSpec((tm, d), lambda i: (i, 0)),
                   pl.BlockSpec((d4, tm // d4, d), lambda i: (0, i, 0)),
                   pl.BlockSpec((d16, tm // d16, d), lambda i: (0, i, 0))],
        out_shape=[jax.ShapeDtypeStruct((m, d), BF16),
                   jax.ShapeDtypeStruct((d4, m // d4, d), BF16),
                   jax.ShapeDtypeStruct((d16, m // d16, d), BF16)],
        compiler_params=_params(1),
        name="rmsnorm_planes",
    )(x, g.reshape(1, d))
    return n, n4.reshape(m, d), n16.reshape(m, d)


def _sigmoid(x):
    return 0.5 * jnp.tanh(0.5 * x) + 0.5


class SideJob(NamedTuple):
    args: tuple
    in_specs: tuple
    out_specs: tuple
    out_shapes: tuple
    body: object


def _flat_step(grid):
    def step(*ids):
        flat = ids[0]
        for extent, i in zip(grid[1:], ids[1:len(grid)]):
            flat = flat * extent + i
        return flat
    return step


def _step_id(grid):
    return _flat_step(grid)(*[pl.program_id(a) for a in range(len(grid))])


def _call_with_sides(body, grid, args, in_specs, out_specs, out_shapes, sides=(), n_prefetch=0,
                     aliases=None, name=None):
    n_in, n_out = len(args) - n_prefetch, len(out_specs)

    def kernel(*refs):
        pos = n_prefetch + n_in
        side_in = []
        for job in sides:
            side_in.append(refs[pos:pos + len(job.args)])
            pos += len(job.args)
        host_out = refs[pos:pos + n_out]
        pos += n_out
        runs = []
        for job, ins in zip(sides, side_in):
            runs.append(job.body(*ins, *refs[pos:pos + len(job.out_specs)]))
            pos += len(job.out_specs)
        runs.append(body(*refs[:n_prefetch + n_in], *host_out))
        active = [run for run in runs if run is not None]
        while active:
            for run in list(active):
                if next(run, StopIteration) is StopIteration:
                    active.remove(run)

    all_args, all_in = list(args), list(in_specs)
    all_out, all_shapes = list(out_specs), list(out_shapes)
    for job in sides:
        all_args += job.args
        all_in += job.in_specs
        all_out += job.out_specs
        all_shapes += job.out_shapes
    if n_prefetch:
        spec = dict(grid_spec=pltpu.PrefetchScalarGridSpec(num_scalar_prefetch=n_prefetch, grid=grid,
                                                           in_specs=all_in, out_specs=all_out))
    else:
        spec = dict(grid=grid, in_specs=all_in, out_specs=all_out)
    outs = pl.pallas_call(kernel, out_shape=all_shapes, input_output_aliases=aliases or {},
                          compiler_params=_params(len(grid)), name=name, **spec)(*all_args)
    host_outs, pos, side_outs = outs[:n_out], n_out, []
    for job in sides:
        side_outs.append(outs[pos:pos + len(job.out_specs)])
        pos += len(job.out_specs)
    return host_outs, side_outs


def _cast_job(w, grid):
    rows, cols = w.shape
    steps = math.prod(grid)
    rb = next(b for b in range(16, rows + 1, 16) if rows % b == 0 and rows // b <= steps)
    last = rows // rb - 1
    flat = _flat_step(grid)
    spec = pl.BlockSpec((rb, cols), lambda *ids: (jnp.minimum(flat(*ids), last), 0))

    def body(src, dst):
        dst[...] = src[...].astype(dst.dtype)

    return SideJob((w,), (spec,), (spec,), (jax.ShapeDtypeStruct((rows, cols), BF16),), body)


def _ffn_up_kernel(x_ref, wg_ref, wu_ref, *rest):
    o_ref = rest[-1]
    x = x_ref[...]
    g = jnp.dot(x, wg_ref[...], preferred_element_type=F32)
    u = jnp.dot(x, wu_ref[...], preferred_element_type=F32)
    o_ref[...] = (g * _sigmoid(g) * u).astype(o_ref.dtype)


FFN_TM = 1024
FFN_TN = 512
DOWN_TN = 512
FFN_TAIL = D_FF % FFN_TN
assert FFN_TAIL % HEAD_DIM == 0 and (D_FF - FFN_TAIL) % FFN_TAIL == 0


def _ffn_up(xn, wg, wu, sides=()):
    m, k = xn.shape
    n = wg.shape[1]
    tm = FFN_TM
    grid = (m // tm, n // FFN_TN)
    x_spec = pl.BlockSpec((tm, k), lambda i, j: (i, 0))
    w_spec = pl.BlockSpec((k, FFN_TN), lambda i, j: (0, j))
    (act,), side_outs = _call_with_sides(
        _ffn_up_kernel, grid, [xn, wg, wu], [x_spec, w_spec, w_spec],
        [pl.BlockSpec((tm, FFN_TN), lambda i, j: (i, j))], [jax.ShapeDtypeStruct((m, n), BF16)],
        sides=[make(grid) for make in sides], name="ffn_up")
    tail_block = (n - FFN_TAIL) // FFN_TAIL
    w_tail = pl.BlockSpec((k, FFN_TAIL), lambda i, j: (0, tail_block))
    act = pl.pallas_call(
        _ffn_up_kernel,
        grid=(m // tm, 1),
        in_specs=[x_spec, w_tail, w_tail, pl.BlockSpec(memory_space=pl.ANY)],
        out_specs=pl.BlockSpec((tm, FFN_TAIL), lambda i, j: (i, tail_block)),
        out_shape=jax.ShapeDtypeStruct((m, n), BF16),
        input_output_aliases={3: 0},
        compiler_params=_params(2),
        name="ffn_up_tail",
    )(xn, wg, wu, act)
    return act, side_outs


def _mm_residual_kernel(x_ref, w_ref, r_ref, *rest, scale):
    o_ref = rest[-1]
    acc = jnp.dot(x_ref[...], w_ref[...], preferred_element_type=F32)
    o_ref[...] = r_ref[...] + scale * acc


def _mm_residual(x, w, res, scale, tm, tn, k_splits=1, out_rows=None, out_row0=0, dst=None, sides=()):
    m, k = x.shape
    n = w.shape[1]
    out_rows = m if out_rows is None else out_rows
    blk0 = out_row0 // tm
    tk = k // k_splits
    grid = (m // tm, n // tn)
    assert out_row0 % tm == 0 and k % k_splits == 0 and tk % HEAD_DIM == 0
    side_outs = []
    for s in range(k_splits):
        first = s == 0
        args = [x, w, res if first else dst]
        in_specs = [pl.BlockSpec((tm, tk), lambda i, j, s=s: (i, s)),
                    pl.BlockSpec((tk, tn), lambda i, j, s=s: (s, j)),
                    pl.BlockSpec((tm, tn), (lambda i, j: (i, j)) if first else (lambda i, j: (i + blk0, j)))]
        aliases = {}
        if first and dst is not None:
            args.append(dst)
            in_specs.append(pl.BlockSpec(memory_space=pl.ANY))
            aliases = {3: 0}
        elif not first:
            aliases = {2: 0}
        (dst,), outs = _call_with_sides(
            functools.partial(_mm_residual_kernel, scale=scale), grid, args, in_specs,
            [pl.BlockSpec((tm, tn), lambda i, j: (i + blk0, j))], [jax.ShapeDtypeStruct((out_rows, n), F32)],
            sides=[make(grid) for make in sides] if first else (), aliases=aliases, name="mm_residual")
        side_outs = outs if first else side_outs
    return dst, side_outs


ROT_HALF = ROT_DIM // 2
PARTNER_LANE = HEAD_DIM // 2


PAIR_TN = 512
ROPE0_TN = 768


def _pair_rotary_lanes_kernel(tbl_ref, w_ref, o_ref):
    src = lax.broadcasted_iota(jnp.int32, (PAIR_TN, PAIR_TN), 0)
    dst = lax.broadcasted_iota(jnp.int32, (PAIR_TN, PAIR_TN), 1)
    lane = dst & (HEAD_DIM - 1)
    shift = PARTNER_LANE - ROT_HALF
    want = dst + jnp.where((lane >= ROT_HALF) & (lane < ROT_DIM), shift, 0) \
               - jnp.where((lane >= PARTNER_LANE) & (lane < PARTNER_LANE + ROT_HALF), shift, 0)
    perm = (src == want).astype(BF16)
    o_ref[...] = jnp.dot(w_ref[...], perm, preferred_element_type=F32).astype(o_ref.dtype)


def _pair_rotary_lanes(w, col_blocks):
    k = w.shape[0]
    grid_spec = pltpu.PrefetchScalarGridSpec(
        num_scalar_prefetch=1,
        grid=(len(col_blocks),),
        in_specs=[pl.BlockSpec((k, PAIR_TN), lambda j, tbl: (0, tbl[j]))],
        out_specs=pl.BlockSpec((k, PAIR_TN), lambda j, tbl: (0, j)))
    return pl.pallas_call(
        _pair_rotary_lanes_kernel,
        grid_spec=grid_spec,
        out_shape=jax.ShapeDtypeStruct((k, len(col_blocks) * PAIR_TN), w.dtype),
        compiler_params=_params(1),
        name="pair_rotary_lanes",
    )(jnp.asarray(col_blocks, jnp.int32), w)


PHASE_COLS = 256
N_PHASES = 4


def _proj_rope_kernel(tbl_ref, x_ref, w_ref, cos_ref, sin_ref, o_ref):
    cos, sin = cos_ref[...], sin_ref[...]
    for c in range(o_ref.shape[1] // PHASE_COLS):
        if c:
            yield
        acc = jnp.dot(x_ref[...], w_ref[:, c * PHASE_COLS:(c + 1) * PHASE_COLS], preferred_element_type=F32)
        for h in range(PHASE_COLS // HEAD_DIM):
            seg = acc[:, h * HEAD_DIM:(h + 1) * HEAD_DIM]
            sl = slice(c * PHASE_COLS + h * HEAD_DIM, c * PHASE_COLS + (h + 1) * HEAD_DIM)
            o_ref[:, sl] = (seg * cos + pltpu.roll(seg, PARTNER_LANE, axis=1) * sin).astype(o_ref.dtype)


def _proj_plain_kernel(tbl_ref, x_ref, w_ref, o_ref):
    o_ref[...] = jnp.dot(x_ref[...], w_ref[...], preferred_element_type=F32).astype(o_ref.dtype)


def _proj_gate_kernel(tbl_ref, x_ref, w_ref, o_ref):
    for c in range(o_ref.shape[1] // PHASE_COLS):
        if c:
            yield
        sl = slice(c * PHASE_COLS, (c + 1) * PHASE_COLS)
        acc = jnp.dot(x_ref[...], w_ref[:, sl], preferred_element_type=F32)
        o_ref[:, sl] = _sigmoid(acc).astype(o_ref.dtype)


def _proj(xn, w, col_blocks, kind, tn, tables=(), tm=1024, sides=()):
    m, k = xn.shape
    assert tn % PHASE_COLS == 0
    body = {"rope": _proj_rope_kernel, "plain": _proj_plain_kernel, "gate": _proj_gate_kernel}[kind]
    grid = (m // tm, len(col_blocks))
    tab = pl.BlockSpec((tm, HEAD_DIM), lambda i, j, tbl: (i, 0))
    (out,), side_outs = _call_with_sides(
        body, grid, [jnp.asarray(col_blocks, jnp.int32), xn, w, *tables],
        [pl.BlockSpec((tm, k), lambda i, j, tbl: (i, 0)),
         pl.BlockSpec((k, tn), lambda i, j, tbl: (0, tbl[j]))] + [tab] * len(tables),
        [pl.BlockSpec((tm, tn), lambda i, j, tbl: (i, j))],
        [jax.ShapeDtypeStruct((m, len(col_blocks) * tn), BF16)],
        sides=[make(grid) for make in sides], n_prefetch=1, name=f"proj_{kind}")
    return out, side_outs


def _seq_bounds(geo, g0):
    lo = jnp.int32(geo.starts[-1])
    hi = jnp.int32(geo.ends[-1])
    for s, e in zip(reversed(geo.starts[:-1]), reversed(geo.ends[:-1])):
        lo = jnp.where(g0 < e, s, lo)
        hi = jnp.where(g0 < e, e, hi)
    return lo, hi


def _band_mask(tq, nk, halo, half_window, lo, hi, u0, stacked=1):
    assert tq & (tq - 1) == 0
    ii = lax.broadcasted_iota(jnp.int32, (stacked * tq, nk), 0) & (tq - 1)
    jj = lax.broadcasted_iota(jnp.int32, (stacked * tq, nk), 1)
    rel = jj - ii - halo
    return ((rel >= -half_window) & (rel <= half_window)
            & (jj >= lo - u0 + halo) & (jj < hi - u0 + halo))


def _mask_bias(valid):
    return jnp.where(valid, 0.0, NEG_INF).astype(F32)


def _phased(work, scores, softmax, values):
    per = -(-len(work) // N_PHASES)
    groups = [work[i:i + per] for i in range(0, len(work), per)]
    s_next = [scores(*item) for item in groups[0]]
    done = []
    for g, items in enumerate(groups):
        yield
        for item, probs in done:
            values(*item, *probs)
        done = [(item, softmax(*item, s)) for item, s in zip(items, s_next)]
        if g + 1 < len(groups):
            s_next = [scores(*item) for item in groups[g + 1]]
    for item, probs in done:
        values(*item, *probs)


A_TQ = 128


def _attn_a_kernel(q_ref, kp_ref, kc_ref, kn_ref, vp_ref, vc_ref, vn_ref, o_ref, l_ref, *, dil, geo, grid):
    step_rows = q_ref.shape[0]
    step0 = _step_id(grid) * step_rows
    plane_rows = geo.m // dil
    nk = A_TQ + 2 * A_HALF
    k = jnp.concatenate([kp_ref[...], kc_ref[...], kn_ref[...]], axis=0)
    v = jnp.concatenate([vp_ref[...], vc_ref[...], vn_ref[...]], axis=0)
    lane = lax.broadcasted_iota(jnp.int32, (A_TQ, HEAD_DIM), 1)
    biases, lse_tiles = {}, {}

    def scores(sub, h):
        if sub not in biases:
            u0 = step0 + sub * A_TQ
            base = (u0 // plane_rows) * plane_rows
            lo, hi = _seq_bounds(geo, (u0 - base) * dil)
            biases[sub] = _mask_bias(_band_mask(A_TQ, nk, A_HALF, A_HALF, base + lo // dil, base + hi // dil, u0))
        sl = slice(h * HEAD_DIM, (h + 1) * HEAD_DIM)
        return lax.dot_general(q_ref[sub * A_TQ:(sub + 1) * A_TQ, sl], k[sub * A_TQ:sub * A_TQ + nk, sl],
                               (((1,), (1,)), ((), ())), preferred_element_type=F32) * (SCALE * LOG2E) + biases[sub]

    def softmax(sub, h, s):
        m = jnp.max(s, axis=-1, keepdims=True)
        p = jnp.exp2(s - m)
        den = jnp.sum(p, axis=-1, keepdims=True)
        lse_tiles[sub] = jnp.where(lane == h, m * (1.0 / LOG2E) + jnp.log(den),
                                   lse_tiles.get(sub, jnp.zeros((A_TQ, HEAD_DIM), F32)))
        if h == A_HEADS - 1:
            l_ref[sub * A_TQ:(sub + 1) * A_TQ, :] = lse_tiles[sub]
        return p.astype(BF16), 1.0 / den

    def values(sub, h, p, inv_den):
        sl = slice(h * HEAD_DIM, (h + 1) * HEAD_DIM)
        o = jnp.dot(p, v[sub * A_TQ:sub * A_TQ + nk, sl], preferred_element_type=F32)
        o_ref[sub * A_TQ:(sub + 1) * A_TQ, sl] = (o * inv_den).astype(o_ref.dtype)

    work = [(sub, h) for sub in range(step_rows // A_TQ) for h in range(A_HEADS)]
    yield from _phased(work, scores, softmax, values)


def _attn_a_job(geo, dil, q_arr, q_blk, k_arr, k_blk, v_arr, v_blk, grid):
    m = geo.m
    step_rows = m // math.prod(grid)
    hb = step_rows // A_HALF
    last_halo = m // A_HALF - 1
    flat = _flat_step(grid)
    assert step_rows % A_TQ == 0 and all(((e - s) // dil) % A_TQ == 0 for s, e in zip(geo.starts, geo.ends))

    def main(width, blk):
        return pl.BlockSpec((step_rows, width), lambda *ids: (flat(*ids), blk))

    def prev(blk):
        return pl.BlockSpec((A_HALF, A_WIDTH), lambda *ids: (jnp.maximum(flat(*ids) * hb - 1, 0), blk))

    def nxt(blk):
        return pl.BlockSpec((A_HALF, A_WIDTH), lambda *ids: (jnp.minimum(flat(*ids) * hb + hb, last_halo), blk))

    return SideJob(
        args=(q_arr, k_arr, k_arr, k_arr, v_arr, v_arr, v_arr),
        in_specs=(main(A_WIDTH, q_blk), prev(k_blk), main(A_WIDTH, k_blk), nxt(k_blk),
                  prev(v_blk), main(A_WIDTH, v_blk), nxt(v_blk)),
        out_specs=(main(A_WIDTH, 0), main(HEAD_DIM, 0)),
        out_shapes=(jax.ShapeDtypeStruct((m, A_WIDTH), BF16), jax.ShapeDtypeStruct((m, HEAD_DIM), F32)),
        body=functools.partial(_attn_a_kernel, dil=dil, geo=geo, grid=grid))


COMBINE_TM = 256


def _combine_a_kernel(o0_ref, l0_ref, o1_ref, l1_ref, o2_ref, l2_ref, y_ref, l1_s, l2_s):
    outs = [o0_ref[...].astype(F32)]
    for o_ref, l_ref, l_s, dil in ((o1_ref, l1_ref, l1_s, DIL_GROUPS[1][1]), (o2_ref, l2_ref, l2_s, DIL_GROUPS[2][1])):
        per = COMBINE_TM // dil
        for r in range(dil):
            l_s[pl.ds(r, per, stride=dil), :] = l_ref[r]
        planes = o_ref[...].reshape(COMBINE_TM, A_WIDTH)
        outs.append(jnp.dot(_plane_permutation(COMBINE_TM, per), planes, preferred_element_type=F32))
    a0, a1, a2 = l0_ref[...], l1_s[...], l2_s[...]
    mx = jnp.maximum(jnp.maximum(a0, a1), a2)
    e0, e1, e2 = jnp.exp(a0 - mx), jnp.exp(a1 - mx), jnp.exp(a2 - mx)
    inv = 1.0 / (e0 + e1 + e2)
    weights = (e0 * inv, e1 * inv, e2 * inv)
    for h in range(A_HEADS):
        sl = slice(h * HEAD_DIM, (h + 1) * HEAD_DIM)
        y = sum(jnp.broadcast_to(w[:, h:h + 1], (COMBINE_TM, HEAD_DIM)) * o[:, sl] for w, o in zip(weights, outs))
        y_ref[:, sl] = y.astype(y_ref.dtype)


def _combine_a(outs, lses):
    m = outs[0].shape[0]
    tm = COMBINE_TM
    d1, d2 = DIL_GROUPS[1][1], DIL_GROUPS[2][1]

    def nat(width):
        return pl.BlockSpec((tm, width), lambda i: (i, 0))

    def planes(dil, width):
        return pl.BlockSpec((dil, tm // dil, width), lambda i: (0, i, 0))

    lse_nat = pltpu.VMEM((tm, HEAD_DIM), F32)
    return pl.pallas_call(
        _combine_a_kernel,
        grid=(m // tm,),
        in_specs=[nat(A_WIDTH), nat(HEAD_DIM), planes(d1, A_WIDTH), planes(d1, HEAD_DIM),
                  planes(d2, A_WIDTH), planes(d2, HEAD_DIM)],
        out_specs=nat(A_WIDTH),
        out_shape=jax.ShapeDtypeStruct((m, A_WIDTH), BF16),
        scratch_shapes=[lse_nat, lse_nat],
        compiler_params=_params(1),
        name="combine_a",
    )(outs[0], lses[0],
      outs[1].reshape(d1, m // d1, A_WIDTH), lses[1].reshape(d1, m // d1, HEAD_DIM),
      outs[2].reshape(d2, m // d2, A_WIDTH), lses[2].reshape(d2, m // d2, HEAD_DIM))


B_TQ = 128


def _attn_b_kernel(sink_ref, q_ref, kp_ref, kc_ref, kn_ref, vp_ref, vc_ref, vn_ref, o_ref, *, geo, grid):
    step_rows = q_ref.shape[0]
    step0 = _step_id(grid) * step_rows
    nk = 3 * B_TQ
    k = jnp.concatenate([kp_ref[...], kc_ref[...], kn_ref[...]], axis=0)
    v = jnp.concatenate([vp_ref[...], vc_ref[...], vn_ref[...]], axis=0)
    row_head = lax.broadcasted_iota(jnp.int32, (B_GROUP * B_TQ, 1), 0) >> (B_TQ.bit_length() - 1)
    biases = {}

    def head_cols(g):
        return [slice((g * B_GROUP + h) * HEAD_DIM, (g * B_GROUP + h + 1) * HEAD_DIM) for h in range(B_GROUP)]

    def scores(sub, g):
        if sub not in biases:
            u0 = step0 + sub * B_TQ
            lo, hi = _seq_bounds(geo, u0)
            biases[sub] = _mask_bias(_band_mask(B_TQ, nk, B_TQ, B_WINDOW, lo, hi, u0, stacked=B_GROUP))
        q = jnp.concatenate([q_ref[sub * B_TQ:(sub + 1) * B_TQ, sl] for sl in head_cols(g)], axis=0)
        return lax.dot_general(q, k[sub * B_TQ:sub * B_TQ + nk, g * HEAD_DIM:(g + 1) * HEAD_DIM],
                               (((1,), (1,)), ((), ())), preferred_element_type=F32) * (SCALE * LOG2E) + biases[sub]

    def softmax(sub, g, s):
        sk = jnp.zeros((B_GROUP * B_TQ, 1), F32)
        for h in range(B_GROUP):
            sk = jnp.where(row_head == h, sink_ref[g * B_GROUP + h] * LOG2E, sk)
        m = jnp.maximum(jnp.max(s, axis=-1, keepdims=True), sk)
        p = jnp.exp2(s - m)
        den = jnp.sum(p, axis=-1, keepdims=True) + jnp.exp2(sk - m)
        return p.astype(BF16), 1.0 / den

    def values(sub, g, p, inv_den):
        o = jnp.dot(p, v[sub * B_TQ:sub * B_TQ + nk, g * HEAD_DIM:(g + 1) * HEAD_DIM],
                    preferred_element_type=F32) * inv_den
        for h, sl in enumerate(head_cols(g)):
            o_ref[sub * B_TQ:(sub + 1) * B_TQ, sl] = o[h * B_TQ:(h + 1) * B_TQ].astype(o_ref.dtype)

    work = [(sub, g) for sub in range(step_rows // B_TQ) for g in range(B_KV_HEADS)]
    yield from _phased(work, scores, softmax, values)


def _attn_b_job(geo, sink, q_arr, q_blk, k_arr, k_blk, v_arr, v_blk, grid):
    m = geo.m
    step_rows = m // math.prod(grid)
    hb = step_rows // B_TQ
    last_halo = m // B_TQ - 1
    flat = _flat_step(grid)
    assert step_rows % B_TQ == 0 and all((e - s) % B_TQ == 0 for s, e in zip(geo.starts, geo.ends))

    def main(width, blk):
        return pl.BlockSpec((step_rows, width), lambda *ids: (flat(*ids), blk))

    def prev(blk):
        return pl.BlockSpec((B_TQ, B_KV_WIDTH), lambda *ids: (jnp.maximum(flat(*ids) * hb - 1, 0), blk))

    def nxt(blk):
        return pl.BlockSpec((B_TQ, B_KV_WIDTH), lambda *ids: (jnp.minimum(flat(*ids) * hb + hb, last_halo), blk))

    return SideJob(
        args=(sink, q_arr, k_arr, k_arr, k_arr, v_arr, v_arr, v_arr),
        in_specs=(pl.BlockSpec(memory_space=pltpu.SMEM), main(B_Q_WIDTH, q_blk),
                  prev(k_blk), main(B_KV_WIDTH, k_blk), nxt(k_blk),
                  prev(v_blk), main(B_KV_WIDTH, v_blk), nxt(v_blk)),
        out_specs=(main(B_Q_WIDTH, 0),),
        out_shapes=(jax.ShapeDtypeStruct((m, B_Q_WIDTH), BF16),),
        body=functools.partial(_attn_b_kernel, geo=geo, grid=grid))


def _merge_kernel(ya_ref, yb_ref, wa_ref, wb_ref, ga_ref, gb_ref, o_ref):
    a = jnp.dot(ya_ref[...], wa_ref[...], preferred_element_type=F32)
    b = jnp.dot(yb_ref[...], wb_ref[...], preferred_element_type=F32)
    o_ref[...] = (ga_ref[...].astype(F32) * a + gb_ref[...].astype(F32) * b).astype(o_ref.dtype)


def _merge(ya, yb, wa, wb, gate_a, gate_b, tm=1024, tn=1024):
    m = ya.shape[0]
    n = wa.shape[1]
    return pl.pallas_call(
        _merge_kernel,
        grid=(m // tm, n // tn),
        in_specs=[pl.BlockSpec((tm, ya.shape[1]), lambda i, j: (i, 0)),
                  pl.BlockSpec((tm, yb.shape[1]), lambda i, j: (i, 0)),
                  pl.BlockSpec((wa.shape[0], tn), lambda i, j: (0, j)),
                  pl.BlockSpec((wb.shape[0], tn), lambda i, j: (0, j)),
                  pl.BlockSpec((tm, tn), lambda i, j: (i, j)),
                  pl.BlockSpec((tm, tn), lambda i, j: (i, j))],
        out_specs=pl.BlockSpec((tm, tn), lambda i, j: (i, j)),
        out_shape=jax.ShapeDtypeStruct((m, n), BF16),
        compiler_params=_params(2),
        name="merge",
    )(ya, yb, wa, wb, gate_a, gate_b)


def _rope_tables(geo, dils):
    half = ROT_HALF
    pos = jnp.concatenate([jnp.arange(e - s) for s, e in zip(geo.starts, geo.ends)])
    inv = ROPE_THETA ** (-jnp.arange(half, dtype=F32) / half)
    ang = pos.astype(F32)[:, None] * inv[None, :]
    cos, sin = jnp.cos(ang), jnp.sin(ang)
    gap = jnp.zeros((geo.m, PARTNER_LANE - half), F32)
    cos_t = jnp.concatenate([cos, gap + 1.0, cos, gap + 1.0], axis=1)
    sin_t = jnp.concatenate([-sin, gap, sin, gap], axis=1)

    def plane_order(t, dil):
        return t.reshape(geo.m // dil, dil, HEAD_DIM).transpose(1, 0, 2).reshape(geo.m, HEAD_DIM)

    return [(plane_order(cos_t, dil), plane_order(sin_t, dil)) for dil in dils]


def _mixers(geo, h, g_mix, wb, sink, later_a, later_b):
    n_nat, n_d4, n_d16 = _rmsnorm_planes(h, g_mix)
    sources = (n_nat, n_d4, n_d16)
    tables = _rope_tables(geo, [dil for _, dil in DIL_GROUPS])

    qa_off, ka_off, va_off, qb_off = 0, 3 * A_WIDTH, 6 * A_WIDTH, 9 * A_WIDTH
    kb_off = qb_off + B_Q_WIDTH
    vb_off = kb_off + B_KV_WIDTH
    gate_off = vb_off + B_KV_WIDTH

    def blocks(start, width, tn):
        assert start % tn == 0 and width % tn == 0
        return tuple(range(start // tn, (start + width) // tn))

    def grp(off, gi, tn):
        return blocks(off + gi * A_WIDTH, A_WIDTH, tn)

    pt = PAIR_TN
    w_rope0 = _pair_rotary_lanes(wb, grp(qa_off, 0, pt) + grp(ka_off, 0, pt)
                                 + blocks(qb_off, B_Q_WIDTH, pt) + blocks(kb_off, B_KV_WIDTH, pt))
    w_rope = _pair_rotary_lanes(wb, grp(qa_off, 1, pt) + grp(ka_off, 1, pt) + grp(qa_off, 2, pt) + grp(ka_off, 2, pt))

    tn0 = B_KV_WIDTH
    rope, ((later_b,),) = _proj(n_nat, w_rope0, blocks(0, w_rope0.shape[1], ROPE0_TN), "rope", ROPE0_TN, tables[0],
                                sides=[lambda grid: _cast_job(later_b, grid)])
    plain, ((later_a,),) = _proj(n_nat, wb, grp(va_off, 0, tn0) + blocks(vb_off, B_KV_WIDTH, tn0), "plain", tn0,
                                 sides=[lambda grid: _cast_job(later_a, grid)])
    rope0, plain0 = rope, plain

    group_outs = []
    for gi in (1, 2):
        dil_prev = DIL_GROUPS[gi - 1][1]
        attend = functools.partial(_attn_a_job, geo, dil_prev, rope, 0, rope, 1, plain, 0)
        rope, (o_l,) = _proj(sources[gi], w_rope, blocks((gi - 1) * 2 * A_WIDTH, 2 * A_WIDTH, A_WIDTH), "rope", A_WIDTH,
                             tables[gi], sides=[attend])
        plain, _ = _proj(sources[gi], wb, grp(va_off, gi, A_WIDTH), "plain", A_WIDTH)
        group_outs.append(o_l)
    attend = functools.partial(_attn_a_job, geo, DIL_GROUPS[2][1], rope, 0, rope, 1, plain, 0)
    gate_a, (o_l,) = _proj(n_nat, wb, blocks(gate_off, D_MODEL, 1024), "gate", 1024, sides=[attend])
    group_outs.append(o_l)
    ya = _combine_a([o for o, _ in group_outs], [l for _, l in group_outs])

    attend = functools.partial(_attn_b_job, geo, sink, rope0, 2 * A_WIDTH // B_Q_WIDTH,
                               rope0, (2 * A_WIDTH + B_Q_WIDTH) // B_KV_WIDTH, plain0, A_WIDTH // B_KV_WIDTH)
    gate_b, ((yb,),) = _proj(n_nat, wb, blocks(gate_off + D_MODEL, D_MODEL, 1024), "gate", 1024, sides=[attend])
    return ya, yb, gate_a, gate_b, later_a, later_b


def _ffn_half_step(parts, total_rows, g, wg, wu, w_down, later_weight=None):
    acts, wd = [], None
    for x in parts:
        sides = [] if wd is not None else [lambda grid: _cast_job(w_down, grid)]
        act, side_outs = _ffn_up(_rmsnorm(x, g, BF16), wg, wu, sides=sides)
        wd = side_outs[0][0] if sides else wd
        acts.append(act)
    out, row0, later = None, 0, None
    for x, act in zip(parts, acts):
        sides = [lambda grid: _cast_job(later_weight, grid)] if x is parts[-1] and later_weight is not None else []
        out, side_outs = _mm_residual(act, wd, x, 0.5, tm=1024, tn=DOWN_TN, k_splits=2, out_rows=total_rows,
                                      out_row0=row0, dst=out, sides=sides)
        later = side_outs[0][0] if sides else later
        row0 += x.shape[0]
    return out, later


def _layer(geo, parts, g_ffn1, w1_gate, w1_up, w1_down, g_mix, w_in, sink_b, w_branch_a, w_branch_b,
           w_out, g_ffn2, w2_gate, w2_up, w2_down, g_final):
    h, w_in_bf16 = _ffn_half_step(parts, geo.m, g_ffn1, w1_gate.astype(BF16), w1_up.astype(BF16), w1_down,
                                  later_weight=w_in)
    ya, yb, gate_a, gate_b, wg2, wu2 = _mixers(geo, h, g_mix, w_in_bf16, sink_b, w2_gate, w2_up)
    merged = _merge(ya, yb, w_branch_a.astype(BF16), w_branch_b.astype(BF16), gate_a, gate_b)
    h2, _ = _mm_residual(merged, w_out.astype(BF16), h, 1.0, tm=1024, tn=1024)
    x2, _ = _ffn_half_step([h2], geo.m, g_ffn2, wg2, wu2, w2_down)
    outs, row0 = [], 0
    for x in parts:
        outs.append(_rmsnorm(x2, g_final, F32, row0=row0, rows=x.shape[0]))
        row0 += x.shape[0]
    return outs


def kernel(x_prompt, x_sample, g_ffn1, w1_gate, w1_up, w1_down, g_mix, w_in, sink_b, w_branch_a, w_branch_b, w_out, g_ffn2, w2_gate, w2_up, w2_down, g_final):
    assert w1_gate.shape[0] == 1, "single layer"
    bp, sp, _ = x_prompt.shape
    bs, ss, _ = x_sample.shape
    lens = (sp,) * bp + (ss,) * bs
    ends = tuple(sum(lens[:i + 1]) for i in range(len(lens)))
    geo = Geometry(starts=(0,) + ends[:-1], ends=ends)
    parts = [x_prompt.reshape(-1, D_MODEL), x_sample.reshape(-1, D_MODEL)]
    y_prompt, y_sample = _layer(
        geo, parts, g_ffn1[0], w1_gate[0], w1_up[0], w1_down[0], g_mix[0], w_in[0], sink_b[0],
        w_branch_a[0], w_branch_b[0], w_out[0], g_ffn2[0], w2_gate[0], w2_up[0], w2_down[0], g_final)
    return (y_prompt.reshape(x_prompt.shape), y_sample.reshape(x_sample.shape))
```

```python
import functools
import math
from typing import NamedTuple

import jax
import jax.numpy as jnp
from jax import lax
from jax.experimental import pallas as pl
from jax.experimental.pallas import tpu as pltpu

F32 = jnp.float32
BF16 = jnp.bfloat16

D_MODEL = 4096
D_FF = 11008
HEAD_DIM = 128
ROT_DIM = HEAD_DIM // 4
ROPE_THETA = 500000.0
DIL_GROUPS = ((128, 1), (512, 4), (2048, 16))
A_HEADS = 8
A_WIDTH = A_HEADS * HEAD_DIM
A_HALF = 64
B_Q_HEADS = 16
B_KV_HEADS = 4
B_GROUP = B_Q_HEADS // B_KV_HEADS
B_Q_WIDTH = B_Q_HEADS * HEAD_DIM
B_KV_WIDTH = B_KV_HEADS * HEAD_DIM
B_WINDOW = 128
EPS = 1e-6
NEG_INF = -1e30
SCALE = 1.0 / math.sqrt(HEAD_DIM)
LOG2E = math.log2(math.e)
assert all((win // 2) // dil == A_HALF for win, dil in DIL_GROUPS)

VMEM_LIMIT = 58 * 1024 * 1024


class Geometry(NamedTuple):
    starts: tuple
    ends: tuple

    @property
    def m(self):
        return self.ends[-1]


def _params(n_grid):
    return pltpu.CompilerParams(
        dimension_semantics=("arbitrary",) * n_grid,
        vmem_limit_bytes=VMEM_LIMIT)


def _normalise(x, g):
    var = jnp.mean(x * x, axis=-1, keepdims=True)
    return (x * lax.rsqrt(var + EPS)) * g


def _rmsnorm_kernel(x_ref, g_ref, o_ref):
    o_ref[...] = _normalise(x_ref[...], g_ref[...]).astype(o_ref.dtype)


def _rmsnorm(x, g, out_dtype, row0=0, rows=None, tm=256):
    d = x.shape[1]
    rows = x.shape[0] if rows is None else rows
    blk0 = row0 // tm
    assert row0 % tm == 0 and rows % tm == 0
    return pl.pallas_call(
        _rmsnorm_kernel,
        grid=(rows // tm,),
        in_specs=[pl.BlockSpec((tm, d), lambda i: (i + blk0, 0)),
                  pl.BlockSpec((1, d), lambda i: (0, 0))],
        out_specs=pl.BlockSpec((tm, d), lambda i: (i, 0)),
        out_shape=jax.ShapeDtypeStruct((rows, d), out_dtype),
        compiler_params=_params(1),
        name="rmsnorm",
    )(x, g.reshape(1, d))


def _rmsnorm_parts_kernel(*refs, bounds):
    g_ref, o_ref = refs[-2:]
    i = pl.program_id(0)
    for x_ref, (lo, hi) in zip(refs[:-2], bounds):
        @pl.when((i >= lo) & (i < hi))
        def _():
            o_ref[...] = _normalise(x_ref[...], g_ref[...]).astype(o_ref.dtype)


def _rmsnorm_parts(parts, g, out_dtype, tm=256):
    d = parts[0].shape[1]
    specs, bounds, lo = [], [], 0
    for part in parts:
        assert part.shape[0] % tm == 0
        nb = part.shape[0] // tm
        specs.append(pl.BlockSpec((tm, d), lambda i, lo=lo, nb=nb: (jnp.clip(i - lo, 0, nb - 1), 0)))
        bounds.append((lo, lo + nb))
        lo += nb
    return pl.pallas_call(
        functools.partial(_rmsnorm_parts_kernel, bounds=bounds),
        grid=(lo,),
        in_specs=specs + [pl.BlockSpec((1, d), lambda i: (0, 0))],
        out_specs=pl.BlockSpec((tm, d), lambda i: (i, 0)),
        out_shape=jax.ShapeDtypeStruct((lo * tm, d), out_dtype),
        compiler_params=_params(1),
        name="rmsnorm_parts",
    )(*parts, g.reshape(1, d))


NORM_TM = 256


def _plane_permutation(tm, dil):
    per = tm // dil
    assert per & (per - 1) == 0, "rows per plane must be a power of two"
    i = lax.broadcasted_iota(jnp.int32, (tm, tm), 0)
    j = lax.broadcasted_iota(jnp.int32, (tm, tm), 1)
    src = (i & (per - 1)) * dil + (i >> (per.bit_length() - 1))
    return (j == src).astype(BF16)


def _rmsnorm_planes_kernel(x_ref, g_ref, n_ref, n4_ref, n16_ref):
    y = _normalise(x_ref[...], g_ref[...]).astype(BF16)
    n_ref[...] = y
    for ref, dil in ((n4_ref, DIL_GROUPS[1][1]), (n16_ref, DIL_GROUPS[2][1])):
        perm = jnp.dot(_plane_permutation(NORM_TM, dil), y, preferred_element_type=F32).astype(BF16)
        ref[...] = perm.reshape(dil, NORM_TM // dil, y.shape[1])


def _rmsnorm_planes(x, g):
    m, d = x.shape
    tm = NORM_TM
    d4, d16 = DIL_GROUPS[1][1], DIL_GROUPS[2][1]
    n, n4, n16 = pl.pallas_call(
        _rmsnorm_planes_kernel,
        grid=(m // tm,),
        in_specs=[pl.BlockSpec((tm, d), lambda i: (i, 0)),
                  pl.BlockSpec((1, d), lambda i: (0, 0))],
        out_specs=[pl.BlockSpec((tm, d), lambda i: (i, 0)),
                   pl.BlockSpec((d4, tm // d4, d), lambda i: (0, i, 0)),
                   pl.BlockSpec((d16, tm // d16, d), lambda i: (0, i, 0))],
        out_shape=[jax.ShapeDtypeStruct((m, d), BF16),
                   jax.ShapeDtypeStruct((d4, m // d4, d), BF16),
                   jax.ShapeDtypeStruct((d16, m // d16, d), BF16)],
        compiler_params=_params(1),
        name="rmsnorm_planes",
    )(x, g.reshape(1, d))
    return n, n4.reshape(m, d), n16.reshape(m, d)


def _sigmoid(x):
    return 0.5 * jnp.tanh(0.5 * x) + 0.5


class SideJob(NamedTuple):
    args: tuple
    in_specs: tuple
    out_specs: tuple
    out_shapes: tuple
    body: object


def _flat_step(grid):
    def step(*ids):
        flat = ids[0]
        for extent, i in zip(grid[1:], ids[1:len(grid)]):
            flat = flat * extent + i
        return flat
    return step


def _step_id(grid):
    return _flat_step(grid)(*[pl.program_id(a) for a in range(len(grid))])


def _call_with_sides(body, grid, args, in_specs, out_specs, out_shapes, sides=(), n_prefetch=0,
                     aliases=None, name=None):
    n_in, n_out = len(args) - n_prefetch, len(out_specs)

    def kernel(*refs):
        pos = n_prefetch + n_in
        side_in = []
        for job in sides:
            side_in.append(refs[pos:pos + len(job.args)])
            pos += len(job.args)
        host_out = refs[pos:pos + n_out]
        pos += n_out
        runs = []
        for job, ins in zip(sides, side_in):
            runs.append(job.body(*ins, *refs[pos:pos + len(job.out_specs)]))
            pos += len(job.out_specs)
        runs.append(body(*refs[:n_prefetch + n_in], *host_out))
        active = [run for run in runs if run is not None]
        while active:
            for run in list(active):
                if next(run, StopIteration) is StopIteration:
                    active.remove(run)

    all_args, all_in = list(args), list(in_specs)
    all_out, all_shapes = list(out_specs), list(out_shapes)
    for job in sides:
        all_args += job.args
        all_in += job.in_specs
        all_out += job.out_specs
        all_shapes += job.out_shapes
    if n_prefetch:
        spec = dict(grid_spec=pltpu.PrefetchScalarGridSpec(num_scalar_prefetch=n_prefetch, grid=grid,
                                                           in_specs=all_in, out_specs=all_out))
    else:
        spec = dict(grid=grid, in_specs=all_in, out_specs=all_out)
    outs = pl.pallas_call(kernel, out_shape=all_shapes, input_output_aliases=aliases or {},
                          compiler_params=_params(len(grid)), name=name, **spec)(*all_args)
    host_outs, pos, side_outs = outs[:n_out], n_out, []
    for job in sides:
        side_outs.append(outs[pos:pos + len(job.out_specs)])
        pos += len(job.out_specs)
    return host_outs, side_outs


def _cast_job(w, grid):
    rows, cols = w.shape
    steps = math.prod(grid)
    rb = next(b for b in range(16, rows + 1, 16) if rows % b == 0 and rows // b <= steps)
    last = rows // rb - 1
    flat = _flat_step(grid)
    spec = pl.BlockSpec((rb, cols), lambda *ids: (jnp.minimum(flat(*ids), last), 0))

    def body(src, dst):
        dst[...] = src[...].astype(dst.dtype)

    return SideJob((w,), (spec,), (spec,), (jax.ShapeDtypeStruct((rows, cols), BF16),), body)


def _ffn_up_kernel(x_ref, wg_ref, wu_ref, *rest):
    o_ref = rest[-1]
    x = x_ref[...]
    g = jnp.dot(x, wg_ref[...], preferred_element_type=F32)
    u = jnp.dot(x, wu_ref[...], preferred_element_type=F32)
    o_ref[...] = (g * _sigmoid(g) * u).astype(o_ref.dtype)


FFN_TM = 1024
FFN_TN = 512
DOWN_TN = 512
FFN_TAIL = D_FF % FFN_TN
assert FFN_TAIL % HEAD_DIM == 0 and (D_FF - FFN_TAIL) % FFN_TAIL == 0


def _ffn_up(xn, wg, wu, sides=()):
    m, k = xn.shape
    n_main = wg.shape[1] - FFN_TAIL
    tm = FFN_TM
    grid = (m // tm, n_main // FFN_TN)
    x_spec = pl.BlockSpec((tm, k), lambda i, j: (i, 0))
    w_spec = pl.BlockSpec((k, FFN_TN), lambda i, j: (0, j))
    (act_main,), side_outs = _call_with_sides(
        _ffn_up_kernel, grid, [xn, wg, wu], [x_spec, w_spec, w_spec],
        [pl.BlockSpec((tm, FFN_TN), lambda i, j: (i, j))], [jax.ShapeDtypeStruct((m, n_main), BF16)],
        sides=[make(grid) for make in sides], name="ffn_up")
    w_tail = pl.BlockSpec((k, FFN_TAIL), lambda i, j: (0, n_main // FFN_TAIL))
    act_tail = pl.pallas_call(
        _ffn_up_kernel,
        grid=(m // tm, 1),
        in_specs=[x_spec, w_tail, w_tail],
        out_specs=pl.BlockSpec((tm, FFN_TAIL), lambda i, j: (i, 0)),
        out_shape=jax.ShapeDtypeStruct((m, FFN_TAIL), BF16),
        compiler_params=_params(2),
        name="ffn_up_tail",
    )(xn, wg, wu)
    return act_main, act_tail, side_outs


def _row_part_specs(parts, tm, tn):
    specs, bounds, lo = [], [], 0
    for part in parts:
        assert part.shape[0] % tm == 0
        nb = part.shape[0] // tm
        specs.append(pl.BlockSpec((tm, tn), lambda i, j, lo=lo, nb=nb: (jnp.clip(i - lo, 0, nb - 1), j)))
        bounds.append((lo, lo + nb))
        lo += nb
    return specs, bounds


def _ffn_down_first_kernel(x_ref, w_ref, *rest, scale, bounds):
    o_ref = rest[-1]
    acc = scale * jnp.dot(x_ref[...], w_ref[...], preferred_element_type=F32)
    i = pl.program_id(0)
    for r_ref, (lo, hi) in zip(rest[:-1], bounds):
        @pl.when((i >= lo) & (i < hi))
        def _():
            o_ref[...] = r_ref[...] + acc


def _ffn_down_second_kernel(x_ref, w_ref, xt_ref, wt_ref, r_ref, o_ref, *, scale):
    acc = jnp.dot(x_ref[...], w_ref[...], preferred_element_type=F32)
    acc = acc + jnp.dot(xt_ref[...], wt_ref[...], preferred_element_type=F32)
    o_ref[...] = r_ref[...] + scale * acc


def _ffn_down(act_main, act_tail, w, res_parts, scale, sides=()):
    m, n_main = act_main.shape
    n = w.shape[1]
    tm, tn, tk = FFN_TM, DOWN_TN, n_main // 2
    grid = (m // tm, n // tn)
    assert tk % HEAD_DIM == 0 and n_main % FFN_TAIL == 0
    tile = pl.BlockSpec((tm, tn), lambda i, j: (i, j))
    res_specs, bounds = _row_part_specs(res_parts, tm, tn)
    first = pl.pallas_call(
        functools.partial(_ffn_down_first_kernel, scale=scale, bounds=bounds),
        grid=grid,
        in_specs=[pl.BlockSpec((tm, tk), lambda i, j: (i, 0)),
                  pl.BlockSpec((tk, tn), lambda i, j: (0, j))] + res_specs,
        out_specs=tile,
        out_shape=jax.ShapeDtypeStruct((m, n), F32),
        compiler_params=_params(2),
        name="ffn_down_first",
    )(act_main, w, *res_parts)
    (out,), side_outs = _call_with_sides(
        functools.partial(_ffn_down_second_kernel, scale=scale), grid,
        [act_main, w, act_tail, w, first],
        [pl.BlockSpec((tm, tk), lambda i, j: (i, 1)),
         pl.BlockSpec((tk, tn), lambda i, j: (1, j)),
         pl.BlockSpec((tm, FFN_TAIL), lambda i, j: (i, 0)),
         pl.BlockSpec((FFN_TAIL, tn), lambda i, j: (n_main // FFN_TAIL, j)),
         tile],
        [tile], [jax.ShapeDtypeStruct((m, n), F32)],
        sides=[make(grid) for make in sides], aliases={4: 0}, name="ffn_down_second")
    return out, side_outs


def _mm_residual_kernel(x_ref, w_ref, r_ref, o_ref):
    o_ref[...] = r_ref[...] + jnp.dot(x_ref[...], w_ref[...], preferred_element_type=F32)


def _mm_residual(x, w, res, tm, tn):
    m, k = x.shape
    n = w.shape[1]
    tile = pl.BlockSpec((tm, tn), lambda i, j: (i, j))
    return pl.pallas_call(
        _mm_residual_kernel,
        grid=(m // tm, n // tn),
        in_specs=[pl.BlockSpec((tm, k), lambda i, j: (i, 0)),
                  pl.BlockSpec((k, tn), lambda i, j: (0, j)),
                  tile],
        out_specs=tile,
        out_shape=jax.ShapeDtypeStruct((m, n), F32),
        compiler_params=_params(2),
        name="mm_residual",
    )(x, w, res)


ROT_HALF = ROT_DIM // 2
PARTNER_LANE = HEAD_DIM // 2


PAIR_TN = 512
ROPE0_TN = 768


def _pair_rotary_lanes_kernel(tbl_ref, w_ref, o_ref):
    src = lax.broadcasted_iota(jnp.int32, (PAIR_TN, PAIR_TN), 0)
    dst = lax.broadcasted_iota(jnp.int32, (PAIR_TN, PAIR_TN), 1)
    lane = dst & (HEAD_DIM - 1)
    shift = PARTNER_LANE - ROT_HALF
    want = dst + jnp.where((lane >= ROT_HALF) & (lane < ROT_DIM), shift, 0) \
               - jnp.where((lane >= PARTNER_LANE) & (lane < PARTNER_LANE + ROT_HALF), shift, 0)
    perm = (src == want).astype(BF16)
    o_ref[...] = jnp.dot(w_ref[...], perm, preferred_element_type=F32).astype(o_ref.dtype)


def _pair_rotary_lanes(w, col_blocks):
    k = w.shape[0]
    grid_spec = pltpu.PrefetchScalarGridSpec(
        num_scalar_prefetch=1,
        grid=(len(col_blocks),),
        in_specs=[pl.BlockSpec((k, PAIR_TN), lambda j, tbl: (0, tbl[j]))],
        out_specs=pl.BlockSpec((k, PAIR_TN), lambda j, tbl: (0, j)))
    return pl.pallas_call(
        _pair_rotary_lanes_kernel,
        grid_spec=grid_spec,
        out_shape=jax.ShapeDtypeStruct((k, len(col_blocks) * PAIR_TN), w.dtype),
        compiler_params=_params(1),
        name="pair_rotary_lanes",
    )(jnp.asarray(col_blocks, jnp.int32), w)


PHASE_COLS = 256
N_PHASES = 4


def _proj_rope_kernel(tbl_ref, x_ref, w_ref, cos_ref, sin_ref, o_ref):
    cos, sin = cos_ref[...], sin_ref[...]
    for c in range(o_ref.shape[1] // PHASE_COLS):
        if c:
            yield
        acc = jnp.dot(x_ref[...], w_ref[:, c * PHASE_COLS:(c + 1) * PHASE_COLS], preferred_element_type=F32)
        for h in range(PHASE_COLS // HEAD_DIM):
            seg = acc[:, h * HEAD_DIM:(h + 1) * HEAD_DIM]
            sl = slice(c * PHASE_COLS + h * HEAD_DIM, c * PHASE_COLS + (h + 1) * HEAD_DIM)
            o_ref[:, sl] = (seg * cos + pltpu.roll(seg, PARTNER_LANE, axis=1) * sin).astype(o_ref.dtype)


def _proj_plain_kernel(tbl_ref, x_ref, w_ref, o_ref):
    o_ref[...] = jnp.dot(x_ref[...], w_ref[...], preferred_element_type=F32).astype(o_ref.dtype)


def _proj_gate_kernel(tbl_ref, x_ref, w_ref, o_ref):
    for c in range(o_ref.shape[1] // PHASE_COLS):
        if c:
            yield
        sl = slice(c * PHASE_COLS, (c + 1) * PHASE_COLS)
        acc = jnp.dot(x_ref[...], w_ref[:, sl], preferred_element_type=F32)
        o_ref[:, sl] = _sigmoid(acc).astype(o_ref.dtype)


def _proj(xn, w, col_blocks, kind, tn, tables=(), tm=1024, sides=()):
    m, k = xn.shape
    assert tn % PHASE_COLS == 0
    body = {"rope": _proj_rope_kernel, "plain": _proj_plain_kernel, "gate": _proj_gate_kernel}[kind]
    grid = (m // tm, len(col_blocks))
    tab = pl.BlockSpec((tm, HEAD_DIM), lambda i, j, tbl: (i, 0))
    (out,), side_outs = _call_with_sides(
        body, grid, [jnp.asarray(col_blocks, jnp.int32), xn, w, *tables],
        [pl.BlockSpec((tm, k), lambda i, j, tbl: (i, 0)),
         pl.BlockSpec((k, tn), lambda i, j, tbl: (0, tbl[j]))] + [tab] * len(tables),
        [pl.BlockSpec((tm, tn), lambda i, j, tbl: (i, j))],
        [jax.ShapeDtypeStruct((m, len(col_blocks) * tn), BF16)],
        sides=[make(grid) for make in sides], n_prefetch=1, name=f"proj_{kind}")
    return out, side_outs


def _seq_bounds(geo, g0):
    lo = jnp.int32(geo.starts[-1])
    hi = jnp.int32(geo.ends[-1])
    for s, e in zip(reversed(geo.starts[:-1]), reversed(geo.ends[:-1])):
        lo = jnp.where(g0 < e, s, lo)
        hi = jnp.where(g0 < e, e, hi)
    return lo, hi


def _band_mask(tq, nk, halo, half_window, lo, hi, u0, stacked=1):
    assert tq & (tq - 1) == 0
    ii = lax.broadcasted_iota(jnp.int32, (stacked * tq, nk), 0) & (tq - 1)
    jj = lax.broadcasted_iota(jnp.int32, (stacked * tq, nk), 1)
    rel = jj - ii - halo
    return ((rel >= -half_window) & (rel <= half_window)
            & (jj >= lo - u0 + halo) & (jj < hi - u0 + halo))


def _mask_bias(valid):
    return jnp.where(valid, 0.0, NEG_INF).astype(F32)


def _phased(work, scores, softmax, values):
    per = -(-len(work) // N_PHASES)
    groups = [work[i:i + per] for i in range(0, len(work), per)]
    s_next = [scores(*item) for item in groups[0]]
    done = []
    for g, items in enumerate(groups):
        yield
        for item, probs in done:
            values(*item, *probs)
        done = [(item, softmax(*item, s)) for item, s in zip(items, s_next)]
        if g + 1 < len(groups):
            s_next = [scores(*item) for item in groups[g + 1]]
    for item, probs in done:
        values(*item, *probs)


A_TQ = 128


def _attn_a_kernel(q_ref, kp_ref, kc_ref, kn_ref, vp_ref, vc_ref, vn_ref, o_ref, l_ref, *, dil, geo, grid):
    step_rows = q_ref.shape[0]
    step0 = _step_id(grid) * step_rows
    plane_rows = geo.m // dil
    nk = A_TQ + 2 * A_HALF
    k = jnp.concatenate([kp_ref[...], kc_ref[...], kn_ref[...]], axis=0)
    v = jnp.concatenate([vp_ref[...], vc_ref[...], vn_ref[...]], axis=0)
    lane = lax.broadcasted_iota(jnp.int32, (A_TQ, HEAD_DIM), 1)
    biases, lse_tiles = {}, {}

    def scores(sub, h):
        if sub not in biases:
            u0 = step0 + sub * A_TQ
            base = (u0 // plane_rows) * plane_rows
            lo, hi = _seq_bounds(geo, (u0 - base) * dil)
            biases[sub] = _mask_bias(_band_mask(A_TQ, nk, A_HALF, A_HALF, base + lo // dil, base + hi // dil, u0))
        sl = slice(h * HEAD_DIM, (h + 1) * HEAD_DIM)
        return lax.dot_general(q_ref[sub * A_TQ:(sub + 1) * A_TQ, sl], k[sub * A_TQ:sub * A_TQ + nk, sl],
                               (((1,), (1,)), ((), ())), preferred_element_type=F32) * (SCALE * LOG2E) + biases[sub]

    def softmax(sub, h, s):
        m = jnp.max(s, axis=-1, keepdims=True)
        p = jnp.exp2(s - m)
        den = jnp.sum(p, axis=-1, keepdims=True)
        lse_tiles[sub] = jnp.where(lane == h, m * (1.0 / LOG2E) + jnp.log(den),
                                   lse_tiles.get(sub, jnp.zeros((A_TQ, HEAD_DIM), F32)))
        if h == A_HEADS - 1:
            l_ref[sub * A_TQ:(sub + 1) * A_TQ, :] = lse_tiles[sub]
        return p.astype(BF16), 1.0 / den

    def values(sub, h, p, inv_den):
        sl = slice(h * HEAD_DIM, (h + 1) * HEAD_DIM)
        o = jnp.dot(p, v[sub * A_TQ:sub * A_TQ + nk, sl], preferred_element_type=F32)
        o_ref[sub * A_TQ:(sub + 1) * A_TQ, sl] = (o * inv_den).astype(o_ref.dtype)

    work = [(sub, h) for sub in range(step_rows // A_TQ) for h in range(A_HEADS)]
    yield from _phased(work, scores, softmax, values)


def _attn_a_job(geo, dil, q_arr, q_blk, k_arr, k_blk, v_arr, v_blk, grid):
    m = geo.m
    step_rows = m // math.prod(grid)
    hb = step_rows // A_HALF
    last_halo = m // A_HALF - 1
    flat = _flat_step(grid)
    assert step_rows % A_TQ == 0 and all(((e - s) // dil) % A_TQ == 0 for s, e in zip(geo.starts, geo.ends))

    def main(width, blk):
        return pl.BlockSpec((step_rows, width), lambda *ids: (flat(*ids), blk))

    def prev(blk):
        return pl.BlockSpec((A_HALF, A_WIDTH), lambda *ids: (jnp.maximum(flat(*ids) * hb - 1, 0), blk))

    def nxt(blk):
        return pl.BlockSpec((A_HALF, A_WIDTH), lambda *ids: (jnp.minimum(flat(*ids) * hb + hb, last_halo), blk))

    return SideJob(
        args=(q_arr, k_arr, k_arr, k_arr, v_arr, v_arr, v_arr),
        in_specs=(main(A_WIDTH, q_blk), prev(k_blk), main(A_WIDTH, k_blk), nxt(k_blk),
                  prev(v_blk), main(A_WIDTH, v_blk), nxt(v_blk)),
        out_specs=(main(A_WIDTH, 0), main(HEAD_DIM, 0)),
        out_shapes=(jax.ShapeDtypeStruct((m, A_WIDTH), BF16), jax.ShapeDtypeStruct((m, HEAD_DIM), F32)),
        body=functools.partial(_attn_a_kernel, dil=dil, geo=geo, grid=grid))


COMBINE_TM = 256


def _combine_a_kernel(o0_ref, l0_ref, o1_ref, l1_ref, o2_ref, l2_ref, y_ref, l1_s, l2_s):
    outs = [o0_ref[...].astype(F32)]
    for o_ref, l_ref, l_s, dil in ((o1_ref, l1_ref, l1_s, DIL_GROUPS[1][1]), (o2_ref, l2_ref, l2_s, DIL_GROUPS[2][1])):
        per = COMBINE_TM // dil
        for r in range(dil):
            l_s[pl.ds(r, per, stride=dil), :] = l_ref[r]
        planes = o_ref[...].reshape(COMBINE_TM, A_WIDTH)
        outs.append(jnp.dot(_plane_permutation(COMBINE_TM, per), planes, preferred_element_type=F32))
    a0, a1, a2 = l0_ref[...], l1_s[...], l2_s[...]
    mx = jnp.maximum(jnp.maximum(a0, a1), a2)
    e0, e1, e2 = jnp.exp(a0 - mx), jnp.exp(a1 - mx), jnp.exp(a2 - mx)
    inv = 1.0 / (e0 + e1 + e2)
    weights = (e0 * inv, e1 * inv, e2 * inv)
    for h in range(A_HEADS):
        sl = slice(h * HEAD_DIM, (h + 1) * HEAD_DIM)
        y = sum(jnp.broadcast_to(w[:, h:h + 1], (COMBINE_TM, HEAD_DIM)) * o[:, sl] for w, o in zip(weights, outs))
        y_ref[:, sl] = y.astype(y_ref.dtype)


def _combine_a(outs, lses):
    m = outs[0].shape[0]
    tm = COMBINE_TM
    d1, d2 = DIL_GROUPS[1][1], DIL_GROUPS[2][1]

    def nat(width):
        return pl.BlockSpec((tm, width), lambda i: (i, 0))

    def planes(dil, width):
        return pl.BlockSpec((dil, tm // dil, width), lambda i: (0, i, 0))

    lse_nat = pltpu.VMEM((tm, HEAD_DIM), F32)
    return pl.pallas_call(
        _combine_a_kernel,
        grid=(m // tm,),
        in_specs=[nat(A_WIDTH), nat(HEAD_DIM), planes(d1, A_WIDTH), planes(d1, HEAD_DIM),
                  planes(d2, A_WIDTH), planes(d2, HEAD_DIM)],
        out_specs=nat(A_WIDTH),
        out_shape=jax.ShapeDtypeStruct((m, A_WIDTH), BF16),
        scratch_shapes=[lse_nat, lse_nat],
        compiler_params=_params(1),
        name="combine_a",
    )(outs[0], lses[0],
      outs[1].reshape(d1, m // d1, A_WIDTH), lses[1].reshape(d1, m // d1, HEAD_DIM),
      outs[2].reshape(d2, m // d2, A_WIDTH), lses[2].reshape(d2, m // d2, HEAD_DIM))


B_TQ = 128


def _attn_b_kernel(sink_ref, q_ref, kp_ref, kc_ref, kn_ref, vp_ref, vc_ref, vn_ref, o_ref, *, geo, grid):
    step_rows = q_ref.shape[0]
    step0 = _step_id(grid) * step_rows
    nk = 3 * B_TQ
    k = jnp.concatenate([kp_ref[...], kc_ref[...], kn_ref[...]], axis=0)
    v = jnp.concatenate([vp_ref[...], vc_ref[...], vn_ref[...]], axis=0)
    row_head = lax.broadcasted_iota(jnp.int32, (B_GROUP * B_TQ, 1), 0) >> (B_TQ.bit_length() - 1)
    biases = {}

    def head_cols(g):
        return [slice((g * B_GROUP + h) * HEAD_DIM, (g * B_GROUP + h + 1) * HEAD_DIM) for h in range(B_GROUP)]

    def scores(sub, g):
        if sub not in biases:
            u0 = step0 + sub * B_TQ
            lo, hi = _seq_bounds(geo, u0)
            biases[sub] = _mask_bias(_band_mask(B_TQ, nk, B_TQ, B_WINDOW, lo, hi, u0, stacked=B_GROUP))
        q = jnp.concatenate([q_ref[sub * B_TQ:(sub + 1) * B_TQ, sl] for sl in head_cols(g)], axis=0)
        return lax.dot_general(q, k[sub * B_TQ:sub * B_TQ + nk, g * HEAD_DIM:(g + 1) * HEAD_DIM],
                               (((1,), (1,)), ((), ())), preferred_element_type=F32) * (SCALE * LOG2E) + biases[sub]

    def softmax(sub, g, s):
        sk = jnp.zeros((B_GROUP * B_TQ, 1), F32)
        for h in range(B_GROUP):
            sk = jnp.where(row_head == h, sink_ref[g * B_GROUP + h] * LOG2E, sk)
        m = jnp.maximum(jnp.max(s, axis=-1, keepdims=True), sk)
        p = jnp.exp2(s - m)
        den = jnp.sum(p, axis=-1, keepdims=True) + jnp.exp2(sk - m)
        return p.astype(BF16), 1.0 / den

    def values(sub, g, p, inv_den):
        o = jnp.dot(p, v[sub * B_TQ:sub * B_TQ + nk, g * HEAD_DIM:(g + 1) * HEAD_DIM],
                    preferred_element_type=F32) * inv_den
        for h, sl in enumerate(head_cols(g)):
            o_ref[sub * B_TQ:(sub + 1) * B_TQ, sl] = o[h * B_TQ:(h + 1) * B_TQ].astype(o_ref.dtype)

    work = [(sub, g) for sub in range(step_rows // B_TQ) for g in range(B_KV_HEADS)]
    yield from _phased(work, scores, softmax, values)


def _attn_b_job(geo, sink, q_arr, q_blk, k_arr, k_blk, v_arr, v_blk, grid):
    m = geo.m
    step_rows = m // math.prod(grid)
    hb = step_rows // B_TQ
    last_halo = m // B_TQ - 1
    flat = _flat_step(grid)
    assert step_rows % B_TQ == 0 and all((e - s) % B_TQ == 0 for s, e in zip(geo.starts, geo.ends))

    def main(width, blk):
        return pl.BlockSpec((step_rows, width), lambda *ids: (flat(*ids), blk))

    def prev(blk):
        return pl.BlockSpec((B_TQ, B_KV_WIDTH), lambda *ids: (jnp.maximum(flat(*ids) * hb - 1, 0), blk))

    def nxt(blk):
        return pl.BlockSpec((B_TQ, B_KV_WIDTH), lambda *ids: (jnp.minimum(flat(*ids) * hb + hb, last_halo), blk))

    return SideJob(
        args=(sink, q_arr, k_arr, k_arr, k_arr, v_arr, v_arr, v_arr),
        in_specs=(pl.BlockSpec(memory_space=pltpu.SMEM), main(B_Q_WIDTH, q_blk),
                  prev(k_blk), main(B_KV_WIDTH, k_blk), nxt(k_blk),
                  prev(v_blk), main(B_KV_WIDTH, v_blk), nxt(v_blk)),
        out_specs=(main(B_Q_WIDTH, 0),),
        out_shapes=(jax.ShapeDtypeStruct((m, B_Q_WIDTH), BF16),),
        body=functools.partial(_attn_b_kernel, geo=geo, grid=grid))


def _merge_kernel(ya_ref, yb_ref, wa_ref, wb_ref, ga_ref, gb_ref, o_ref):
    a = jnp.dot(ya_ref[...], wa_ref[...], preferred_element_type=F32)
    b = jnp.dot(yb_ref[...], wb_ref[...], preferred_element_type=F32)
    o_ref[...] = (ga_ref[...].astype(F32) * a + gb_ref[...].astype(F32) * b).astype(o_ref.dtype)


def _merge(ya, yb, wa, wb, gate_a, gate_b, tm=1024, tn=1024):
    m = ya.shape[0]
    n = wa.shape[1]
    return pl.pallas_call(
        _merge_kernel,
        grid=(m // tm, n // tn),
        in_specs=[pl.BlockSpec((tm, ya.shape[1]), lambda i, j: (i, 0)),
                  pl.BlockSpec((tm, yb.shape[1]), lambda i, j: (i, 0)),
                  pl.BlockSpec((wa.shape[0], tn), lambda i, j: (0, j)),
                  pl.BlockSpec((wb.shape[0], tn), lambda i, j: (0, j)),
                  pl.BlockSpec((tm, tn), lambda i, j: (i, j)),
                  pl.BlockSpec((tm, tn), lambda i, j: (i, j))],
        out_specs=pl.BlockSpec((tm, tn), lambda i, j: (i, j)),
        out_shape=jax.ShapeDtypeStruct((m, n), BF16),
        compiler_params=_params(2),
        name="merge",
    )(ya, yb, wa, wb, gate_a, gate_b)


def _rope_tables(geo, dils):
    half = ROT_HALF
    pos = jnp.concatenate([jnp.arange(e - s) for s, e in zip(geo.starts, geo.ends)])
    inv = ROPE_THETA ** (-jnp.arange(half, dtype=F32) / half)
    ang = pos.astype(F32)[:, None] * inv[None, :]
    cos, sin = jnp.cos(ang), jnp.sin(ang)
    gap = jnp.zeros((geo.m, PARTNER_LANE - half), F32)
    cos_t = jnp.concatenate([cos, gap + 1.0, cos, gap + 1.0], axis=1)
    sin_t = jnp.concatenate([-sin, gap, sin, gap], axis=1)

    def plane_order(t, dil):
        return t.reshape(geo.m // dil, dil, HEAD_DIM).transpose(1, 0, 2).reshape(geo.m, HEAD_DIM)

    return [(plane_order(cos_t, dil), plane_order(sin_t, dil)) for dil in dils]


def _mixers(geo, h, g_mix, wb, sink, later_a, later_b):
    n_nat, n_d4, n_d16 = _rmsnorm_planes(h, g_mix)
    sources = (n_nat, n_d4, n_d16)
    tables = _rope_tables(geo, [dil for _, dil in DIL_GROUPS])

    qa_off, ka_off, va_off, qb_off = 0, 3 * A_WIDTH, 6 * A_WIDTH, 9 * A_WIDTH
    kb_off = qb_off + B_Q_WIDTH
    vb_off = kb_off + B_KV_WIDTH
    gate_off = vb_off + B_KV_WIDTH

    def blocks(start, width, tn):
        assert start % tn == 0 and width % tn == 0
        return tuple(range(start // tn, (start + width) // tn))

    def grp(off, gi, tn):
        return blocks(off + gi * A_WIDTH, A_WIDTH, tn)

    pt = PAIR_TN
    w_rope0 = _pair_rotary_lanes(wb, grp(qa_off, 0, pt) + grp(ka_off, 0, pt)
                                 + blocks(qb_off, B_Q_WIDTH, pt) + blocks(kb_off, B_KV_WIDTH, pt))
    w_rope = _pair_rotary_lanes(wb, grp(qa_off, 1, pt) + grp(ka_off, 1, pt) + grp(qa_off, 2, pt) + grp(ka_off, 2, pt))

    tn0 = B_KV_WIDTH
    rope, ((later_b,),) = _proj(n_nat, w_rope0, blocks(0, w_rope0.shape[1], ROPE0_TN), "rope", ROPE0_TN, tables[0],
                                sides=[lambda grid: _cast_job(later_b, grid)])
    plain, ((later_a,),) = _proj(n_nat, wb, grp(va_off, 0, tn0) + blocks(vb_off, B_KV_WIDTH, tn0), "plain", tn0,
                                 sides=[lambda grid: _cast_job(later_a, grid)])
    rope0, plain0 = rope, plain

    group_outs = []
    for gi in (1, 2):
        dil_prev = DIL_GROUPS[gi - 1][1]
        attend = functools.partial(_attn_a_job, geo, dil_prev, rope, 0, rope, 1, plain, 0)
        rope, (o_l,) = _proj(sources[gi], w_rope, blocks((gi - 1) * 2 * A_WIDTH, 2 * A_WIDTH, A_WIDTH), "rope", A_WIDTH,
                             tables[gi], sides=[attend])
        plain, _ = _proj(sources[gi], wb, grp(va_off, gi, A_WIDTH), "plain", A_WIDTH)
        group_outs.append(o_l)
    attend = functools.partial(_attn_a_job, geo, DIL_GROUPS[2][1], rope, 0, rope, 1, plain, 0)
    gate_a, (o_l,) = _proj(n_nat, wb, blocks(gate_off, D_MODEL, 1024), "gate", 1024, sides=[attend])
    group_outs.append(o_l)
    ya = _combine_a([o for o, _ in group_outs], [l for _, l in group_outs])

    attend = functools.partial(_attn_b_job, geo, sink, rope0, 2 * A_WIDTH // B_Q_WIDTH,
                               rope0, (2 * A_WIDTH + B_Q_WIDTH) // B_KV_WIDTH, plain0, A_WIDTH // B_KV_WIDTH)
    gate_b, ((yb,),) = _proj(n_nat, wb, blocks(gate_off + D_MODEL, D_MODEL, 1024), "gate", 1024, sides=[attend])
    return ya, yb, gate_a, gate_b, later_a, later_b


def _ffn_half_step(parts, g, wg, wu, w_down, later_weight=None):
    xn = _rmsnorm_parts(parts, g, BF16)
    act_main, act_tail, ((wd,),) = _ffn_up(xn, wg, wu, sides=[lambda grid: _cast_job(w_down, grid)])
    sides = [] if later_weight is None else [lambda grid: _cast_job(later_weight, grid)]
    out, side_outs = _ffn_down(act_main, act_tail, wd, parts, 0.5, sides=sides)
    return out, (side_outs[0][0] if sides else None)


def _layer(geo, parts, g_ffn1, w1_gate, w1_up, w1_down, g_mix, w_in, sink_b, w_branch_a, w_branch_b,
           w_out, g_ffn2, w2_gate, w2_up, w2_down, g_final):
    h, w_in_bf16 = _ffn_half_step(parts, g_ffn1, w1_gate.astype(BF16), w1_up.astype(BF16), w1_down,
                                  later_weight=w_in)
    ya, yb, gate_a, gate_b, wg2, wu2 = _mixers(geo, h, g_mix, w_in_bf16, sink_b, w2_gate, w2_up)
    merged = _merge(ya, yb, w_branch_a.astype(BF16), w_branch_b.astype(BF16), gate_a, gate_b)
    h2 = _mm_residual(merged, w_out.astype(BF16), h, tm=1024, tn=1024)
    x2, _ = _ffn_half_step([h2], g_ffn2, wg2, wu2, w2_down)
    outs, row0 = [], 0
    for x in parts:
        outs.append(_rmsnorm(x2, g_final, F32, row0=row0, rows=x.shape[0]))
        row0 += x.shape[0]
    return outs


def kernel(x_prompt, x_sample, g_ffn1, w1_gate, w1_up, w1_down, g_mix, w_in, sink_b, w_branch_a, w_branch_b, w_out, g_ffn2, w2_gate, w2_up, w2_down, g_final):
    assert w1_gate.shape[0] == 1, "single layer"
    bp, sp, _ = x_prompt.shape
    bs, ss, _ = x_sample.shape
    lens = (sp,) * bp + (ss,) * bs
    ends = tuple(sum(lens[:i + 1]) for i in range(len(lens)))
    geo = Geometry(starts=(0,) + ends[:-1], ends=ends)
    parts = [x_prompt.reshape(-1, D_MODEL), x_sample.reshape(-1, D_MODEL)]
    y_prompt, y_sample = _layer(
        geo, parts, g_ffn1[0], w1_gate[0], w1_up[0], w1_down[0], g_mix[0], w_in[0], sink_b[0],
        w_branch_a[0], w_branch_b[0], w_out[0], g_ffn2[0], w2_gate[0], w2_up[0], w2_down[0], g_final)
    return (y_prompt.reshape(x_prompt.shape), y_sample.reshape(x_sample.shape))
```
